```python
import jax, jax.numpy as jnp
from jax import lax
import numpy as np

D_MODEL = 4096
BATCH = 2
SEQ = 4096
DEPTH = 2

CHUNK = 64
CONV_K = 4
ML_WIDTH = D_MODEL // 2
ML_HEADS = 4
ML_DV = ML_WIDTH // ML_HEADS
ML_DK = ML_DV // 2
GDN_WIDTH = D_MODEL // 2
GDN_HEAD_DIM = 128
GDN_HEADS = GDN_WIDTH // GDN_HEAD_DIM
SSM_WIDTH = D_MODEL
SSM_HEAD_DIM = 64
SSM_HEADS = SSM_WIDTH // SSM_HEAD_DIM
SSM_GROUPS = 8
SSM_STATE = 128
MIX_WIDTH = ML_WIDTH + GDN_WIDTH + SSM_WIDTH

DEEPNORM_ALPHA = (2 * DEPTH) ** 0.25
DEEPNORM_BETA = (8 * DEPTH) ** -0.25
RMS_EPS = 1e-6
LN_EPS = 1e-5

ML_SPLITS = [ML_HEADS * ML_DK, ML_HEADS * ML_DK, ML_WIDTH, ML_WIDTH, ML_WIDTH, ML_HEADS, ML_HEADS]
GDN_SPLITS = [GDN_WIDTH, GDN_WIDTH, GDN_WIDTH, GDN_WIDTH, GDN_HEADS, GDN_HEADS]
SSM_SPLITS = [SSM_WIDTH, SSM_WIDTH, SSM_GROUPS * SSM_STATE, SSM_GROUPS * SSM_STATE, SSM_HEADS]
ML_VALUE = [False, False, True, False, False, False, False]
GDN_VALUE = [False, False, True, False, False, False]
SSM_VALUE = [False, True, False, False, False]
ML_COLS = sum(ML_SPLITS)
GDN_COLS = sum(GDN_SPLITS)
SSM_COLS = sum(SSM_SPLITS)
IN_COLS = ML_COLS + GDN_COLS + SSM_COLS

kernel_name = "hybrid_mlstm_gdn_mamba2_deepnorm"


def _split(u, sizes):
    idx = np.cumsum(sizes)[:-1].tolist()
    return jnp.split(u, idx, axis=-1)


def _to_chunks(a):
    b, s = a.shape[:2]
    return jnp.moveaxis(a.reshape(b, s // CHUNK, CHUNK, *a.shape[2:]), 1, 0)


def _from_chunks(a):
    nc, b, l = a.shape[:3]
    return jnp.moveaxis(a, 0, 1).reshape(b, nc * l, *a.shape[3:])


def _rms(u, w):
    return u * lax.rsqrt(jnp.mean(u * u, -1, keepdims=True) + RMS_EPS) * w


def _l2n(u):
    return u * lax.rsqrt(jnp.sum(u * u, -1, keepdims=True) + RMS_EPS)


def _layernorm(u, g, b):
    u = u.astype(jnp.float32)
    mu = jnp.mean(u, -1, keepdims=True)
    var = jnp.mean(jnp.square(u - mu), -1, keepdims=True)
    return (u - mu) * lax.rsqrt(var + LN_EPS) * g.astype(jnp.float32) + b.astype(jnp.float32)


def _causal_conv(u, w):
    return lax.conv_general_dilated(u, w[:, None, :], (1,), [(w.shape[0] - 1, 0)],
                                    dimension_numbers=("NWC", "WIO", "NWC"),
                                    feature_group_count=u.shape[-1])


def _masks():
    causal = jnp.tril(jnp.ones((CHUNK, CHUNK), bool))
    strict = jnp.tril(jnp.ones((CHUNK, CHUNK), bool), k=-1)
    return causal, strict


def _mlstm_group(q, k, v, o, z, i_raw, f_raw, i_bias, f_bias, norm_w):
    f32 = jnp.float32
    b, s, _ = q.shape
    causal, _ = _masks()
    q = q.astype(f32).reshape(b, s, ML_HEADS, ML_DK) * ML_DK ** -0.5
    k = k.astype(f32).reshape(b, s, ML_HEADS, ML_DK)
    v = v.astype(f32).reshape(b, s, ML_HEADS, ML_DV)
    li = i_raw.astype(f32) + i_bias.astype(f32)
    lf = jax.nn.log_sigmoid(f_raw.astype(f32) + f_bias.astype(f32))

    def step(carry, inp):
        C, n, m = carry
        q_, k_, v_, li_, lf_ = inp
        bcum = jnp.cumsum(lf_, axis=1).transpose(0, 2, 1)
        li_h = li_.transpose(0, 2, 1)
        Dlog = jnp.where(causal, bcum[..., :, None] - bcum[..., None, :] + li_h[..., None, :], -jnp.inf)
        inter = bcum + m[..., None]
        m_t = jnp.maximum(inter, Dlog.max(-1))
        w_intra = jnp.exp(Dlog - m_t[..., None])
        w_inter = jnp.exp(inter - m_t)
        sc = jnp.einsum("blhd,bshd->bhls", q_, k_) * w_intra
        num = (jnp.einsum("bhls,bshv->blhv", sc, v_)
               + w_inter.transpose(0, 2, 1)[..., None] * jnp.einsum("blhd,bhdv->blhv", q_, C))
        den = sc.sum(-1) + w_inter * jnp.einsum("blhd,bhd->bhl", q_, n)
        h = num / jnp.maximum(jnp.abs(den), jnp.exp(-m_t)).transpose(0, 2, 1)[..., None]
        bL = bcum[..., -1]
        g_s = bL[..., None] - bcum + li_h
        m_new = jnp.maximum(bL + m, g_s.max(-1))
        ws = jnp.exp(g_s - m_new[..., None])
        dec = jnp.exp(bL + m - m_new)
        C = dec[..., None, None] * C + jnp.einsum("bhs,bshd,bshv->bhdv", ws, k_, v_)
        n = dec[..., None] * n + jnp.einsum("bhs,bshd->bhd", ws, k_)
        return (C, n, m_new), h

    init = (jnp.zeros((b, ML_HEADS, ML_DK, ML_DV), f32), jnp.zeros((b, ML_HEADS, ML_DK), f32),
            jnp.zeros((b, ML_HEADS), f32))
    _, h = lax.scan(step, init, (_to_chunks(q), _to_chunks(k), _to_chunks(v), _to_chunks(li), _to_chunks(lf)))
    h = _from_chunks(h)
    h = _rms(h, norm_w.astype(f32).reshape(ML_HEADS, ML_DV)).reshape(b, s, ML_WIDTH)
    return h * jax.nn.sigmoid(o.astype(f32)) * jax.nn.silu(z.astype(f32))


def _gdn_group(q, k, v, z, beta_raw, a_raw, conv_w, A_log, dt_bias, norm_w):
    f32 = jnp.float32
    b, s, _ = q.shape
    causal, strict = _masks()
    qkv = jax.nn.silu(_causal_conv(jnp.concatenate([q, k, v], -1).astype(f32), conv_w.astype(f32)))
    q, k, v = jnp.split(qkv, 3, axis=-1)
    q = _l2n(q.reshape(b, s, GDN_HEADS, GDN_HEAD_DIM)) * GDN_HEAD_DIM ** -0.5
    k = _l2n(k.reshape(b, s, GDN_HEADS, GDN_HEAD_DIM))
    v = v.reshape(b, s, GDN_HEADS, GDN_HEAD_DIM)
    beta = jax.nn.sigmoid(beta_raw.astype(f32))
    g = -jnp.exp(A_log.astype(f32)) * jax.nn.softplus(a_raw.astype(f32) + dt_bias.astype(f32))

    def step(S, inp):
        q_, k_, v_, b_, g_ = inp
        q_, k_, v_ = (t.transpose(0, 2, 1, 3) for t in (q_, k_, v_))
        b_, g_ = b_.transpose(0, 2, 1), g_.transpose(0, 2, 1)
        gc = jnp.cumsum(g_, -1)
        decay = jnp.exp(jnp.where(causal, gc[..., :, None] - gc[..., None, :], -jnp.inf))
        A = jnp.where(strict, b_[..., :, None] * jnp.einsum("bhld,bhsd->bhls", k_, k_) * decay, 0.0)
        rhs = jnp.concatenate([v_ * b_[..., None], k_ * (b_ * jnp.exp(gc))[..., None]], -1)
        X = lax.linalg.triangular_solve(A, rhs, left_side=True, lower=True, unit_diagonal=True)
        u, w = X[..., :GDN_HEAD_DIM], X[..., GDN_HEAD_DIM:]
        v_new = u - jnp.einsum("bhlk,bhkv->bhlv", w, S)
        attn = jnp.einsum("bhlk,bhsk->bhls", q_, k_) * decay
        o = (jnp.einsum("bhlk,bhkv->bhlv", q_ * jnp.exp(gc)[..., None], S)
             + jnp.einsum("bhls,bhsv->bhlv", attn, v_new))
        gL = gc[..., -1:]
        S = S * jnp.exp(gL)[..., None] + jnp.einsum("bhsk,bhsv->bhkv", k_ * jnp.exp(gL - gc)[..., None], v_new)
        return S, o.transpose(0, 2, 1, 3)

    init = jnp.zeros((b, GDN_HEADS, GDN_HEAD_DIM, GDN_HEAD_DIM), f32)
    _, o = lax.scan(step, init, (_to_chunks(q), _to_chunks(k), _to_chunks(v), _to_chunks(beta), _to_chunks(g)))
    o = _rms(_from_chunks(o), norm_w.astype(f32))
    return o.reshape(b, s, GDN_WIDTH) * jax.nn.silu(z.astype(f32))


def _ssm_group(z, xs, Bm, Cm, dt_raw, conv_w, conv_b, A_log, dt_bias, Dskip, norm_w):
    f32 = jnp.float32
    b, s, _ = z.shape
    causal, _ = _masks()
    R = SSM_HEADS // SSM_GROUPS
    xbc = jax.nn.silu(_causal_conv(jnp.concatenate([xs, Bm, Cm], -1).astype(f32), conv_w.astype(f32))
                      + conv_b.astype(f32))
    x, Bm, Cm = _split(xbc, [SSM_WIDTH, SSM_GROUPS * SSM_STATE, SSM_GROUPS * SSM_STATE])
    x = x.reshape(b, s, SSM_HEADS, SSM_HEAD_DIM)
    Bm = Bm.reshape(b, s, SSM_GROUPS, SSM_STATE)
    Cm = Cm.reshape(b, s, SSM_GROUPS, SSM_STATE)
    dt = jax.nn.softplus(dt_raw.astype(f32) + dt_bias.astype(f32))
    a = dt * (-jnp.exp(A_log.astype(f32)))
    xdt = x * dt[..., None]

    def step(st, inp):
        x_, a_, B_, C_ = inp
        acs = jnp.cumsum(a_, axis=1).transpose(0, 2, 1)
        Lm = jnp.exp(jnp.where(causal, acs[..., :, None] - acs[..., None, :], -jnp.inf))
        Lg = Lm.reshape(b, SSM_GROUPS, R, CHUNK, CHUNK)
        xg = x_.reshape(b, CHUNK, SSM_GROUPS, R, SSM_HEAD_DIM)
        CB = jnp.einsum("blgn,bsgn->bgls", C_, B_)
        y_diag = jnp.einsum("bgls,bgrls,bsgrp->blgrp", CB, Lg, xg)
        sg = st.reshape(b, SSM_GROUPS, R, SSM_HEAD_DIM, SSM_STATE)
        y_off = jnp.einsum("blgn,bgrpn,bgrl->blgrp", C_, sg, jnp.exp(acs).reshape(b, SSM_GROUPS, R, CHUNK))
        aL = acs[..., -1]
        wdec = jnp.exp(aL[..., None] - acs).reshape(b, SSM_GROUPS, R, CHUNK)
        new = jnp.einsum("bsgn,bgrs,bsgrp->bgrpn", B_, wdec, xg).reshape(st.shape)
        st = st * jnp.exp(aL)[..., None, None] + new
        return st, (y_diag + y_off).reshape(b, CHUNK, SSM_HEADS, SSM_HEAD_DIM)

    init = jnp.zeros((b, SSM_HEADS, SSM_HEAD_DIM, SSM_STATE), f32)
    _, y = lax.scan(step, init, (_to_chunks(xdt), _to_chunks(a), _to_chunks(Bm), _to_chunks(Cm)))
    y = _from_chunks(y) + Dskip.astype(f32)[:, None] * x
    y = y.reshape(b, s, SSM_WIDTH) * jax.nn.silu(z.astype(f32))
    y = _rms(y.reshape(b, s, SSM_GROUPS, SSM_WIDTH // SSM_GROUPS),
             norm_w.astype(f32).reshape(SSM_GROUPS, SSM_WIDTH // SSM_GROUPS))
    return y.reshape(b, s, SSM_WIDTH)


def setup_inputs(seed: int = 0) -> dict:
    key = jax.random.key(seed)
    ks = jax.random.split(key, 20)
    f32 = jnp.float32

    def inv_softplus_dt(k, n):
        dt = jnp.exp(jax.random.uniform(k, (DEPTH, n), f32, np.log(1e-3), np.log(1e-1)))
        return dt + jnp.log(-jnp.expm1(-dt))

    segs = list(zip(ML_SPLITS + GDN_SPLITS + SSM_SPLITS, ML_VALUE + GDN_VALUE + SSM_VALUE))
    col_scale = jnp.concatenate([jnp.full((n,), DEEPNORM_BETA if isv else 1.0, f32) for n, isv in segs])
    x = jax.random.normal(ks[0], (BATCH, SEQ, D_MODEL), f32)
    w_in = jax.random.normal(ks[1], (DEPTH, D_MODEL, IN_COLS), f32) * (D_MODEL ** -0.5) * col_scale
    w_out = jax.random.normal(ks[2], (DEPTH, MIX_WIDTH, D_MODEL), f32) * (MIX_WIDTH ** -0.5) * DEEPNORM_BETA
    ml_i_bias = -2.0 + 0.1 * jax.random.normal(ks[3], (DEPTH, ML_HEADS), f32)
    ml_f_bias = jnp.linspace(3.0, 6.0, ML_HEADS, dtype=f32) + 0.1 * jax.random.normal(ks[4], (DEPTH, ML_HEADS), f32)
    ml_norm_w = 1.0 + 0.02 * jax.random.normal(ks[5], (DEPTH, ML_WIDTH), f32)
    gdn_conv_w = jax.random.normal(ks[6], (DEPTH, CONV_K, 3 * GDN_WIDTH), f32) * CONV_K ** -0.5
    gdn_A_log = jnp.log(jax.random.uniform(ks[7], (DEPTH, GDN_HEADS), f32, 1.0, 16.0))
    gdn_dt_bias = inv_softplus_dt(ks[8], GDN_HEADS)
    gdn_norm_w = 1.0 + 0.02 * jax.random.normal(ks[9], (DEPTH, GDN_HEAD_DIM), f32)
    conv_ch = SSM_WIDTH + 2 * SSM_GROUPS * SSM_STATE
    ssm_conv_w = jax.random.normal(ks[10], (DEPTH, CONV_K, conv_ch), f32) * CONV_K ** -0.5
    ssm_conv_b = 0.02 * jax.random.normal(ks[11], (DEPTH, conv_ch), f32)
    ssm_A_log = jnp.log(jax.random.uniform(ks[12], (DEPTH, SSM_HEADS), f32, 1.0, 16.0))
    ssm_dt_bias = inv_softplus_dt(ks[13], SSM_HEADS)
    ssm_D = 1.0 + 0.1 * jax.random.normal(ks[14], (DEPTH, SSM_HEADS), f32)
    ssm_norm_w = 1.0 + 0.02 * jax.random.normal(ks[15], (DEPTH, SSM_WIDTH), f32)
    ln_g = 1.0 + 0.02 * jax.random.normal(ks[16], (DEPTH, D_MODEL), f32)
    ln_b = 0.02 * jax.random.normal(ks[17], (DEPTH, D_MODEL), f32)
    return {"x": x, "w_in": w_in, "w_out": w_out, "ml_i_bias": ml_i_bias, "ml_f_bias": ml_f_bias,
            "ml_norm_w": ml_norm_w, "gdn_conv_w": gdn_conv_w, "gdn_A_log": gdn_A_log,
            "gdn_dt_bias": gdn_dt_bias, "gdn_norm_w": gdn_norm_w, "ssm_conv_w": ssm_conv_w,
            "ssm_conv_b": ssm_conv_b, "ssm_A_log": ssm_A_log, "ssm_dt_bias": ssm_dt_bias,
            "ssm_D": ssm_D, "ssm_norm_w": ssm_norm_w, "ln_g": ln_g, "ln_b": ln_b}


def reference(x, w_in, w_out, ml_i_bias, ml_f_bias, ml_norm_w, gdn_conv_w, gdn_A_log, gdn_dt_bias,
              gdn_norm_w, ssm_conv_w, ssm_conv_b, ssm_A_log, ssm_dt_bias, ssm_D, ssm_norm_w, ln_g, ln_b):
    for l in range(DEPTH):
        proj = jnp.einsum("bsd,dn->bsn", x, w_in[l])
        ml_p, gdn_p, ssm_p = _split(proj, [ML_COLS, GDN_COLS, SSM_COLS])
        y_ml = _mlstm_group(*_split(ml_p, ML_SPLITS), ml_i_bias[l], ml_f_bias[l], ml_norm_w[l])
        y_gdn = _gdn_group(*_split(gdn_p, GDN_SPLITS), gdn_conv_w[l], gdn_A_log[l], gdn_dt_bias[l], gdn_norm_w[l])
        y_ssm = _ssm_group(*_split(ssm_p, SSM_SPLITS), ssm_conv_w[l], ssm_conv_b[l], ssm_A_log[l],
                           ssm_dt_bias[l], ssm_D[l], ssm_norm_w[l])
        y_mix = jnp.concatenate([y_ml, y_gdn, y_ssm], -1).astype(x.dtype)
        y = jnp.einsum("bsm,md->bsd", y_mix, w_out[l])
        x = _layernorm(DEEPNORM_ALPHA * x.astype(jnp.float32) + y.astype(jnp.float32), ln_g[l], ln_b[l]).astype(x.dtype)
    return x
```

```python
import functools

import jax
import jax.numpy as jnp
from jax import lax
from jax.experimental import pallas as pl
from jax.experimental.pallas import tpu as pltpu

F32 = jnp.float32
BF16 = jnp.bfloat16
HIGHEST = lax.Precision.HIGHEST

D_MODEL = 4096
DEPTH = 2
CHUNK = 64
CONV_K = 4
ML_WIDTH = D_MODEL // 2
ML_HEADS = 4
ML_DV = ML_WIDTH // ML_HEADS
ML_DK = ML_DV // 2
GDN_WIDTH = D_MODEL // 2
GDN_HEAD_DIM = 128
GDN_HEADS = GDN_WIDTH // GDN_HEAD_DIM
SSM_WIDTH = D_MODEL
SSM_HEAD_DIM = 64
SSM_HEADS = SSM_WIDTH // SSM_HEAD_DIM
SSM_GROUPS = 8
SSM_STATE = 128
SSM_GROUP_WIDTH = SSM_WIDTH // SSM_GROUPS
SSM_GROUP_HEADS = SSM_HEADS // SSM_GROUPS
MIX_WIDTH = ML_WIDTH + GDN_WIDTH + SSM_WIDTH
DEEPNORM_ALPHA = (2 * DEPTH) ** 0.25
RMS_EPS = 1e-6
LN_EPS = 1e-5
NEG_BIG = -1e30

ML_MAIN = ML_HEADS * ML_DK * 2 + 3 * ML_WIDTH
ML_COLS = ML_MAIN + 2 * ML_HEADS
GDN_MAIN = 4 * GDN_WIDTH
GDN_COLS = GDN_MAIN + 2 * GDN_HEADS
SSM_MAIN = 2 * SSM_WIDTH + 2 * SSM_GROUPS * SSM_STATE
SSM_COLS = SSM_MAIN + SSM_HEADS
MAIN_COLS = ML_MAIN + GDN_MAIN + SSM_MAIN

P_ML_Q = 0
P_ML_K = P_ML_Q + ML_HEADS * ML_DK
P_ML_V = P_ML_K + ML_HEADS * ML_DK
P_ML_O = P_ML_V + ML_WIDTH
P_ML_Z = P_ML_O + ML_WIDTH
P_GDN_Q = ML_MAIN
P_GDN_K = P_GDN_Q + GDN_WIDTH
P_GDN_V = P_GDN_K + GDN_WIDTH
P_GDN_Z = P_GDN_V + GDN_WIDTH
P_SSM_Z = ML_MAIN + GDN_MAIN
P_SSM_X = P_SSM_Z + SSM_WIDTH
P_SSM_B = P_SSM_X + SSM_WIDTH
P_SSM_C = P_SSM_B + SSM_GROUPS * SSM_STATE

GATE_COLS = 128
G_ML_I = 0
G_ML_F = G_ML_I + ML_HEADS
G_GDN_BETA = G_ML_F + ML_HEADS
G_GDN_A = G_GDN_BETA + GDN_HEADS
G_SSM_DT = G_GDN_A + GDN_HEADS
G_END = G_SSM_DT + SSM_HEADS

HALO = 8
SEQ_BLOCK = 512
VMEM_LIMIT = 56 * 1024 * 1024


def _sigmoid(u):
    return 1.0 / (1.0 + jnp.exp(-u))


def _softplus(u):
    return jnp.maximum(u, 0.0) + jnp.log1p(jnp.exp(-jnp.abs(u)))


def _dot(a, b):
    return jnp.dot(a.astype(BF16), b.astype(BF16), preferred_element_type=F32)


def _dot_nt(a, b):
    return lax.dot_general(a.astype(BF16), b.astype(BF16), (((1,), (1,)), ((), ())),
                           preferred_element_type=F32)


def _dot_tn(a, b):
    return lax.dot_general(a.astype(BF16), b.astype(BF16), (((0,), (0,)), ((), ())),
                           preferred_element_type=F32)


def _rms_rows(u, w_row):
    return u * lax.rsqrt(jnp.mean(u * u, axis=-1, keepdims=True) + RMS_EPS) * w_row


def _chunk_masks():
    ri = lax.broadcasted_iota(jnp.int32, (CHUNK, CHUNK), 0)
    ci = lax.broadcasted_iota(jnp.int32, (CHUNK, CHUNK), 1)
    return ci <= ri, ci < ri, ci == ri, ri <= ci


def _gate_column(g, col):
    lane = lax.broadcasted_iota(jnp.int32, g.shape, 1)
    return jnp.sum(jnp.where(lane == col, g, 0.0), axis=1, keepdims=True)


def _mm_kernel(x_ref, w_ref, o_ref, *scratch, nk):
    part = jnp.dot(x_ref[...], w_ref[...], preferred_element_type=F32)
    if nk == 1:
        o_ref[...] = part.astype(o_ref.dtype)
        return
    acc_ref, = scratch
    k = pl.program_id(2)

    @pl.when(k == 0)
    def _():
        acc_ref[...] = part

    @pl.when(k > 0)
    def _():
        acc_ref[...] += part

    @pl.when(k == nk - 1)
    def _():
        o_ref[...] = acc_ref[...].astype(o_ref.dtype)


def _matmul(x, w, out_dtype, tm, tn, tk):
    m, kdim = x.shape
    n = w.shape[1]
    tm, tn, tk = min(tm, m), min(tn, n), min(tk, kdim)
    nk = kdim // tk
    scratch = [] if nk == 1 else [pltpu.VMEM((tm, tn), F32)]
    return pl.pallas_call(
        functools.partial(_mm_kernel, nk=nk),
        grid=(n // tn, m // tm, nk),
        in_specs=[pl.BlockSpec((tm, tk), lambda j, i, k: (i, k)),
                  pl.BlockSpec((tk, tn), lambda j, i, k: (k, j))],
        out_specs=pl.BlockSpec((tm, tn), lambda j, i, k: (i, j)),
        out_shape=jax.ShapeDtypeStruct((m, n), out_dtype),
        scratch_shapes=scratch,
        compiler_params=pltpu.CompilerParams(
            dimension_semantics=("parallel", "parallel", "arbitrary"),
            vmem_limit_bytes=VMEM_LIMIT),
        name="proj_matmul",
    )(x, w)


def _gates_kernel(x_ref, w_ref, bias_ref, alog_ref, o_ref):
    raw = jnp.dot(x_ref[...], w_ref[...], preferred_element_type=F32, precision=HIGHEST)
    col = lax.broadcasted_iota(jnp.int32, raw.shape, 1)
    u = raw + bias_ref[...]
    sp = _softplus(u)
    out = jnp.where(col < G_ML_F, u,
          jnp.where(col < G_GDN_BETA, u - sp,
          jnp.where(col < G_GDN_A, _sigmoid(u),
          jnp.where(col < G_SSM_DT, -jnp.exp(alog_ref[...]) * sp,
                    sp))))
    o_ref[...] = out


def _gates(x, w_gate, bias_row, alog_row, tm=512):
    m, kdim = x.shape
    tm = min(tm, m)
    return pl.pallas_call(
        _gates_kernel,
        grid=(m // tm,),
        in_specs=[pl.BlockSpec((tm, kdim), lambda i: (i, 0)),
                  pl.BlockSpec((kdim, GATE_COLS), lambda i: (0, 0)),
                  pl.BlockSpec((1, GATE_COLS), lambda i: (0, 0)),
                  pl.BlockSpec((1, GATE_COLS), lambda i: (0, 0))],
        out_specs=pl.BlockSpec((tm, GATE_COLS), lambda i: (i, 0)),
        out_shape=jax.ShapeDtypeStruct((m, GATE_COLS), F32),
        compiler_params=pltpu.CompilerParams(dimension_semantics=("parallel",),
                                             vmem_limit_bytes=VMEM_LIMIT),
        name="gate_proj",
    )(x, w_gate, bias_row, alog_row)


def _mlstm_kernel(q_ref, k_ref, v_ref, o_ref, z_ref, g_ref, nw_ref, out_ref,
                  c_ref, n_ref, m_ref, li_ref, lf_ref, *, ts):
    head = pl.program_id(1)

    @pl.when(pl.program_id(2) == 0)
    def _():
        c_ref[...] = jnp.zeros_like(c_ref)
        n_ref[...] = jnp.zeros_like(n_ref)
        m_ref[...] = jnp.zeros_like(m_ref)

    g = g_ref[...]
    li_ref[...] = _gate_column(g, G_ML_I + head)
    lf_ref[...] = _gate_column(g, G_ML_F + head)
    causal, _, eye, upper = _chunk_masks()
    nw = nw_ref[...]

    def chunk(c, carry):
        r0 = pl.multiple_of(c * CHUNK, CHUNK)
        rows = pl.ds(r0, CHUNK)
        q = q_ref[rows, :] * (ML_DK ** -0.5)
        k = k_ref[rows, :]
        v = v_ref[rows, :]
        li_c = li_ref[rows, :]
        lf_c = lf_ref[rows, :]
        li_r = jnp.sum(jnp.where(eye, li_c, 0.0), axis=0, keepdims=True)
        lf_r = jnp.sum(jnp.where(eye, lf_c, 0.0), axis=0, keepdims=True)
        bcum_c = jnp.sum(jnp.where(causal, lf_r, 0.0), axis=1, keepdims=True)
        bcum_r = jnp.sum(jnp.where(upper, lf_c, 0.0), axis=0, keepdims=True)
        m_prev = m_ref[...]
        dlog = jnp.where(causal, bcum_c - bcum_r + li_r, NEG_BIG)
        inter = bcum_c + m_prev
        m_t = jnp.maximum(inter, jnp.max(dlog, axis=1, keepdims=True))
        w_intra = jnp.exp(dlog - m_t)
        w_inter = jnp.exp(inter - m_t)
        sc = _dot_nt(q, k) * w_intra
        num = _dot(sc, v) + w_inter * _dot(q, c_ref[...])
        den = (jnp.sum(sc, axis=1, keepdims=True)
               + w_inter * jnp.sum(q * n_ref[...], axis=1, keepdims=True))
        hval = num / jnp.maximum(jnp.abs(den), jnp.exp(-m_t))
        b_last = bcum_c[CHUNK - 1:CHUNK, :]
        gs_c = b_last - bcum_c + li_c
        gs_r = b_last - bcum_r + li_r
        m_new = jnp.maximum(b_last + m_prev, jnp.max(gs_r, axis=1, keepdims=True))
        ws_c = jnp.exp(gs_c - m_new)
        dec = jnp.exp(b_last + m_prev - m_new)
        kw = k * ws_c
        c_ref[...] = dec * c_ref[...] + _dot_tn(kw, v)
        n_ref[...] = dec * n_ref[...] + jnp.sum(kw, axis=0, keepdims=True)
        m_ref[...] = m_new
        y = _rms_rows(hval, nw)
        zz = z_ref[rows, :]
        y = y * _sigmoid(o_ref[rows, :]) * (zz * _sigmoid(zz))
        out_ref[rows, :] = y.astype(out_ref.dtype)
        return carry

    lax.fori_loop(0, ts // CHUNK, chunk, 0)


def _mlstm(proj, gact, norm_w, batch, seq):
    ts = min(SEQ_BLOCK, seq)
    ns = seq // ts
    row = lambda b, h, s: b * ns + s
    qk_blk = lambda off: pl.BlockSpec((ts, ML_DK), lambda b, h, s: (row(b, h, s), off // ML_DK + h))
    v_blk = lambda off: pl.BlockSpec((ts, ML_DV), lambda b, h, s: (row(b, h, s), off // ML_DV + h))
    return pl.pallas_call(
        functools.partial(_mlstm_kernel, ts=ts),
        grid=(batch, ML_HEADS, ns),
        in_specs=[qk_blk(P_ML_Q), qk_blk(P_ML_K), v_blk(P_ML_V), v_blk(P_ML_O), v_blk(P_ML_Z),
                  pl.BlockSpec((ts, GATE_COLS), lambda b, h, s: (row(b, h, s), 0)),
                  pl.BlockSpec((1, ML_DV), lambda b, h, s: (0, h))],
        out_specs=pl.BlockSpec((ts, ML_DV), lambda b, h, s: (row(b, h, s), h)),
        out_shape=jax.ShapeDtypeStruct((batch * seq, ML_WIDTH), BF16),
        scratch_shapes=[pltpu.VMEM((ML_DK, ML_DV), F32), pltpu.VMEM((1, ML_DK), F32),
                        pltpu.VMEM((1, 1), F32), pltpu.VMEM((ts, 1), F32), pltpu.VMEM((ts, 1), F32)],
        compiler_params=pltpu.CompilerParams(
            dimension_semantics=("parallel", "parallel", "arbitrary"),
            vmem_limit_bytes=VMEM_LIMIT),
        name="mlstm_mixer",
    )(proj, proj, proj, proj, proj, gact, norm_w.reshape(1, ML_WIDTH))


def _causal_conv_block(cur_ref, halo_ref, w_ref, buf_ref, first, ts):
    cur = cur_ref[...]
    buf_ref[0:HALO, :] = jnp.where(first, 0.0, halo_ref[...])
    buf_ref[HALO:HALO + ts, :] = cur
    acc = w_ref[CONV_K - 1:CONV_K, :] * cur
    for tap in range(CONV_K - 1):
        acc = acc + w_ref[tap:tap + 1, :] * buf_ref[pl.ds(HALO - (CONV_K - 1) + tap, ts), :]
    return acc


def _gdn_kernel(q_ref, k_ref, v_ref, qh_ref, kh_ref, vh_ref, z_ref, g_ref,
                wq_ref, wk_ref, wv_ref, nw_ref, out_ref,
                s_ref, buf_ref, qs_ref, ks_ref, vs_ref, beta_ref, gl_ref, *, ts):
    head = pl.program_id(1)
    first = pl.program_id(2) == 0

    @pl.when(first)
    def _():
        s_ref[...] = jnp.zeros_like(s_ref)

    def l2n(u):
        return u * lax.rsqrt(jnp.sum(u * u, axis=-1, keepdims=True) + RMS_EPS)

    qc = _causal_conv_block(q_ref, qh_ref, wq_ref, buf_ref, first, ts)
    qs_ref[...] = l2n(qc * _sigmoid(qc)) * (GDN_HEAD_DIM ** -0.5)
    kc = _causal_conv_block(k_ref, kh_ref, wk_ref, buf_ref, first, ts)
    ks_ref[...] = l2n(kc * _sigmoid(kc))
    vc = _causal_conv_block(v_ref, vh_ref, wv_ref, buf_ref, first, ts)
    vs_ref[...] = vc * _sigmoid(vc)
    g = g_ref[...]
    beta_ref[...] = _gate_column(g, G_GDN_BETA + head)
    gl_ref[...] = _gate_column(g, G_GDN_A + head)
    causal, strict, eye, upper = _chunk_masks()
    eye_f = eye.astype(F32)
    nw = nw_ref[...]

    def chunk(c, carry):
        r0 = pl.multiple_of(c * CHUNK, CHUNK)
        rows = pl.ds(r0, CHUNK)
        q = qs_ref[rows, :]
        k = ks_ref[rows, :]
        v = vs_ref[rows, :]
        b_c = beta_ref[rows, :]
        g_c = gl_ref[rows, :]
        g_r = jnp.sum(jnp.where(eye, g_c, 0.0), axis=0, keepdims=True)
        gc_c = jnp.sum(jnp.where(causal, g_r, 0.0), axis=1, keepdims=True)
        gc_r = jnp.sum(jnp.where(upper, g_c, 0.0), axis=0, keepdims=True)
        decay = jnp.exp(jnp.where(causal, gc_c - gc_r, NEG_BIG))
        kq = jnp.concatenate([k, q], axis=0)
        kk_qk = _dot_nt(kq, k)
        a = jnp.where(strict, b_c * kk_qk[:CHUNK] * decay, 0.0)
        tinv = eye_f - a
        apow = a
        for step in range(5):
            apow = _dot(apow, apow)
            tinv = tinv + _dot(tinv, apow)
        state = s_ref[...]
        kq_s = _dot(kq, state)
        eg_c = jnp.exp(gc_c)
        v_new = _dot(tinv, b_c * (v - eg_c * kq_s[:CHUNK]))
        attn = kk_qk[CHUNK:] * decay
        o = eg_c * kq_s[CHUNK:] + _dot(attn, v_new)
        g_last = gc_c[CHUNK - 1:CHUNK, :]
        s_ref[...] = state * jnp.exp(g_last) + _dot_tn(k * jnp.exp(g_last - gc_c), v_new)
        zz = z_ref[rows, :]
        y = _rms_rows(o, nw) * (zz * _sigmoid(zz))
        out_ref[rows, :] = y.astype(out_ref.dtype)
        return carry

    lax.fori_loop(0, ts // CHUNK, chunk, 0)


def _halo_row(b, s, ns, ts):
    return jnp.maximum((b * ns + s) * (ts // HALO) - 1, 0)


def _gdn(proj, gact, conv_w, norm_w, batch, seq):
    ts = min(SEQ_BLOCK, seq)
    ns = seq // ts
    d = GDN_HEAD_DIM
    row = lambda b, h, s: b * ns + s
    cur = lambda off: pl.BlockSpec((ts, d), lambda b, h, s: (row(b, h, s), off // d + h))
    halo = lambda off: pl.BlockSpec((HALO, d), lambda b, h, s: (_halo_row(b, s, ns, ts), off // d + h))
    cw = lambda off: pl.BlockSpec((CONV_K, d), lambda b, h, s: (0, off // d + h))
    return pl.pallas_call(
        functools.partial(_gdn_kernel, ts=ts),
        grid=(batch, GDN_HEADS, ns),
        in_specs=[cur(P_GDN_Q), cur(P_GDN_K), cur(P_GDN_V),
                  halo(P_GDN_Q), halo(P_GDN_K), halo(P_GDN_V),
                  cur(P_GDN_Z),
                  pl.BlockSpec((ts, GATE_COLS), lambda b, h, s: (row(b, h, s), 0)),
                  cw(0), cw(GDN_WIDTH), cw(2 * GDN_WIDTH),
                  pl.BlockSpec((1, d), lambda b, h, s: (0, 0))],
        out_specs=pl.BlockSpec((ts, d), lambda b, h, s: (row(b, h, s), h)),
        out_shape=jax.ShapeDtypeStruct((batch * seq, GDN_WIDTH), BF16),
        scratch_shapes=[pltpu.VMEM((d, d), F32), pltpu.VMEM((HALO + ts, d), F32),
                        pltpu.VMEM((ts, d), F32), pltpu.VMEM((ts, d), F32), pltpu.VMEM((ts, d), F32),
                        pltpu.VMEM((ts, 1), F32), pltpu.VMEM((ts, 1), F32)],
        compiler_params=pltpu.CompilerParams(
            dimension_semantics=("parallel", "parallel", "arbitrary"),
            vmem_limit_bytes=VMEM_LIMIT),
        name="gdn_mixer",
    )(proj, proj, proj, proj, proj, proj, proj, gact, conv_w, conv_w, conv_w, norm_w.reshape(1, d))


def _ssd_kernel(z_ref, x_ref, b_ref, c_ref, xh_ref, bh_ref, ch_ref, g_ref,
                wx_ref, wb_ref, wc_ref, bx_ref, bb_ref, bc_ref, alog_ref, dskip_ref, nw_ref, out_ref,
                st_ref, bufx_ref, bufn_ref, xs_ref, bs_ref, cs_ref, dt_ref, *, ts):
    group = pl.program_id(1)
    first = pl.program_id(2) == 0
    gw = SSM_GROUP_WIDTH
    hd = SSM_HEAD_DIM

    @pl.when(first)
    def _():
        st_ref[...] = jnp.zeros_like(st_ref)

    xc = _causal_conv_block(x_ref, xh_ref, wx_ref, bufx_ref, first, ts) + bx_ref[...]
    xs_ref[...] = xc * _sigmoid(xc)
    bc = _causal_conv_block(b_ref, bh_ref, wb_ref, bufn_ref, first, ts) + bb_ref[...]
    bs_ref[...] = bc * _sigmoid(bc)
    cc = _causal_conv_block(c_ref, ch_ref, wc_ref, bufn_ref, first, ts) + bc_ref[...]
    cs_ref[...] = cc * _sigmoid(cc)

    sel_r = lax.broadcasted_iota(jnp.int32, (GATE_COLS, gw), 0)
    sel_c = lax.broadcasted_iota(jnp.int32, (GATE_COLS, gw), 1)
    sel = (sel_r == G_SSM_DT + group * SSM_GROUP_HEADS + sel_c // hd).astype(F32)
    dt_ref[...] = jnp.dot(g_ref[...], sel, preferred_element_type=F32, precision=HIGHEST)

    neg_a = -jnp.exp(alog_ref[...])
    dskip = dskip_ref[...]
    nw = nw_ref[...]
    ri = lax.broadcasted_iota(jnp.int32, (CHUNK, gw), 0)
    cm = lax.broadcasted_iota(jnp.int32, (CHUNK, gw), 1) % hd
    causal_w = cm <= ri
    upper_w = ri <= cm
    tri_f = _chunk_masks()[0].astype(F32)
    lane_lo = lax.broadcasted_iota(jnp.int32, (CHUNK, 2 * hd), 1) < hd

    def chunk(c, carry):
        r0 = pl.multiple_of(c * CHUNK, CHUNK)
        rows = pl.ds(r0, CHUNK)
        x = xs_ref[rows, :]
        bm = bs_ref[rows, :]
        cmat = cs_ref[rows, :]
        dt = dt_ref[rows, :]
        a = dt * neg_a
        acs = jnp.dot(tri_f, a, preferred_element_type=F32, precision=HIGHEST)
        acs_r = jnp.sum(jnp.where(upper_w, a, 0.0), axis=0, keepdims=True)
        a_last = acs[CHUNK - 1:CHUNK, :]
        lm = jnp.exp(jnp.where(causal_w, acs - acs_r, NEG_BIG))
        cb2 = _dot_nt(cmat, jnp.concatenate([bm, bm], axis=0))
        mm = (jnp.concatenate([cb2] * (gw // (2 * hd)), axis=1) * lm).astype(BF16)
        xdt = x * dt
        y_parts = []
        for p in range(gw // (2 * hd)):
            xp = xdt[:, p * 2 * hd:(p + 1) * 2 * hd]
            rhs = jnp.concatenate([jnp.where(lane_lo, xp, 0.0), jnp.where(lane_lo, 0.0, xp)], axis=0)
            y_parts.append(_dot(mm[:, p * 2 * hd:(p + 1) * 2 * hd], rhs))
        y = jnp.concatenate(y_parts, axis=1)
        state = st_ref[...]
        y = y + _dot(cmat, state) * jnp.exp(acs)
        st_ref[...] = state * jnp.exp(a_last) + _dot_tn(bm, xdt * jnp.exp(a_last - acs))
        y = y + dskip * x
        zz = z_ref[rows, :]
        y = y * (zz * _sigmoid(zz))
        out_ref[rows, :] = _rms_rows(y, nw).astype(out_ref.dtype)
        return carry

    lax.fori_loop(0, ts // CHUNK, chunk, 0)


def _ssd(proj, gact, conv_w, conv_b, a_log, dskip, norm_w, batch, seq):
    ts = min(SEQ_BLOCK, seq)
    ns = seq // ts
    gw, n = SSM_GROUP_WIDTH, SSM_STATE
    row = lambda b, g, s: b * ns + s
    cur = lambda off, w: pl.BlockSpec((ts, w), lambda b, g, s: (row(b, g, s), off // w + g))
    halo = lambda off, w: pl.BlockSpec((HALO, w), lambda b, g, s: (_halo_row(b, s, ns, ts), off // w + g))
    par = lambda rows_, off, w: pl.BlockSpec((rows_, w), lambda b, g, s: (0, off // w + g))
    expand = lambda p: jnp.repeat(p, SSM_HEAD_DIM).reshape(1, SSM_WIDTH)
    conv_b = conv_b.reshape(1, -1)
    return pl.pallas_call(
        functools.partial(_ssd_kernel, ts=ts),
        grid=(batch, SSM_GROUPS, ns),
        in_specs=[cur(P_SSM_Z, gw), cur(P_SSM_X, gw), cur(P_SSM_B, n), cur(P_SSM_C, n),
                  halo(P_SSM_X, gw), halo(P_SSM_B, n), halo(P_SSM_C, n),
                  pl.BlockSpec((ts, GATE_COLS), lambda b, g, s: (row(b, g, s), 0)),
                  par(CONV_K, 0, gw), par(CONV_K, SSM_WIDTH, n), par(CONV_K, SSM_WIDTH + SSM_GROUPS * n, n),
                  par(1, 0, gw), par(1, SSM_WIDTH, n), par(1, SSM_WIDTH + SSM_GROUPS * n, n),
                  par(1, 0, gw), par(1, 0, gw), par(1, 0, gw)],
        out_specs=pl.BlockSpec((ts, gw), lambda b, g, s: (row(b, g, s), g)),
        out_shape=jax.ShapeDtypeStruct((batch * seq, SSM_WIDTH), BF16),
        scratch_shapes=[pltpu.VMEM((n, gw), F32), pltpu.VMEM((HALO + ts, gw), F32),
                        pltpu.VMEM((HALO + ts, n), F32),
                        pltpu.VMEM((ts, gw), F32), pltpu.VMEM((ts, n), F32), pltpu.VMEM((ts, n), F32),
                        pltpu.VMEM((ts, gw), F32)],
        compiler_params=pltpu.CompilerParams(
            dimension_semantics=("parallel", "parallel", "arbitrary"),
            vmem_limit_bytes=VMEM_LIMIT),
        name="ssd_mixer",
    )(proj, proj, proj, proj, proj, proj, proj, gact, conv_w, conv_w, conv_w, conv_b, conv_b, conv_b,
      expand(a_log), expand(dskip), norm_w.reshape(1, SSM_WIDTH))


def _deepnorm_kernel(y_ref, x_ref, g_ref, b_ref, o_ref, obf_ref):
    u = DEEPNORM_ALPHA * x_ref[...] + y_ref[...]
    mu = jnp.mean(u, axis=-1, keepdims=True)
    var = jnp.mean(jnp.square(u - mu), axis=-1, keepdims=True)
    out = (u - mu) * lax.rsqrt(var + LN_EPS) * g_ref[...] + b_ref[...]
    o_ref[...] = out
    obf_ref[...] = out.astype(BF16)


def _deepnorm(y, x, ln_g, ln_b, tm=256):
    m, d = x.shape
    tm = min(tm, m)
    blk = pl.BlockSpec((tm, d), lambda i: (i, 0))
    vec = pl.BlockSpec((1, d), lambda i: (0, 0))
    return pl.pallas_call(
        _deepnorm_kernel,
        grid=(m // tm,),
        in_specs=[blk, blk, vec, vec],
        out_specs=[blk, blk],
        out_shape=[jax.ShapeDtypeStruct((m, d), F32), jax.ShapeDtypeStruct((m, d), BF16)],
        compiler_params=pltpu.CompilerParams(dimension_semantics=("parallel",),
                                             vmem_limit_bytes=VMEM_LIMIT),
        name="deepnorm_ln",
    )(y, x, ln_g.reshape(1, d), ln_b.reshape(1, d))


def _pack_in_weights(w):
    gdn0 = ML_COLS
    ssm0 = ML_COLS + GDN_COLS
    main = jnp.concatenate([w[:, :ML_MAIN], w[:, gdn0:gdn0 + GDN_MAIN], w[:, ssm0:ssm0 + SSM_MAIN]],
                           axis=1).astype(BF16)
    gate = jnp.concatenate([w[:, ML_MAIN:ML_COLS], w[:, gdn0 + GDN_MAIN:ssm0], w[:, ssm0 + SSM_MAIN:],
                            jnp.zeros((w.shape[0], GATE_COLS - G_END), w.dtype)], axis=1)
    return main, gate


def _gate_rows(ml_i_bias, ml_f_bias, gdn_dt_bias, gdn_a_log, ssm_dt_bias):
    zeros = lambda n: jnp.zeros((n,), F32)
    bias = jnp.concatenate([ml_i_bias, ml_f_bias, zeros(GDN_HEADS), gdn_dt_bias, ssm_dt_bias,
                            zeros(GATE_COLS - G_END)])
    alog = jnp.concatenate([zeros(G_GDN_A), gdn_a_log, zeros(GATE_COLS - G_SSM_DT)])
    return bias.reshape(1, GATE_COLS), alog.reshape(1, GATE_COLS)


def kernel(x, w_in, w_out, ml_i_bias, ml_f_bias, ml_norm_w, gdn_conv_w, gdn_A_log, gdn_dt_bias,
           gdn_norm_w, ssm_conv_w, ssm_conv_b, ssm_A_log, ssm_dt_bias, ssm_D, ssm_norm_w, ln_g, ln_b):
    batch, seq, d = x.shape
    xf = x.reshape(batch * seq, d).astype(F32)
    xb = xf.astype(BF16)
    for l in range(w_in.shape[0]):
        w_main, w_gate = _pack_in_weights(w_in[l])
        bias_row, alog_row = _gate_rows(ml_i_bias[l], ml_f_bias[l], gdn_dt_bias[l], gdn_A_log[l],
                                        ssm_dt_bias[l])
        proj = _matmul(xb, w_main, F32, 1024, 1024, d)
        gact = _gates(xf, w_gate, bias_row, alog_row)
        y_ml = _mlstm(proj, gact, ml_norm_w[l], batch, seq)
        y_gdn = _gdn(proj, gact, gdn_conv_w[l], gdn_norm_w[l], batch, seq)
        y_ssm = _ssd(proj, gact, ssm_conv_w[l], ssm_conv_b[l], ssm_A_log[l], ssm_D[l], ssm_norm_w[l],
                     batch, seq)
        y_mix = jnp.concatenate([y_ml, y_gdn, y_ssm], axis=1)
        y = _matmul(y_mix, w_out[l].astype(BF16), F32, 1024, 1024, 2048)
        xf, xb = _deepnorm(y, xf, ln_g[l], ln_b[l])
    return xf.reshape(batch, seq, d).astype(x.dtype)
```

```python
import functools

import jax
import jax.numpy as jnp
from jax import lax
from jax.experimental import pallas as pl
from jax.experimental.pallas import tpu as pltpu

F32 = jnp.float32
BF16 = jnp.bfloat16
HIGHEST = lax.Precision.HIGHEST

D_MODEL = 4096
DEPTH = 2
CHUNK = 64
CONV_K = 4
ML_WIDTH = D_MODEL // 2
ML_HEADS = 4
ML_DV = ML_WIDTH // ML_HEADS
ML_DK = ML_DV // 2
GDN_WIDTH = D_MODEL // 2
GDN_HEAD_DIM = 128
GDN_HEADS = GDN_WIDTH // GDN_HEAD_DIM
SSM_WIDTH = D_MODEL
SSM_HEAD_DIM = 64
SSM_HEADS = SSM_WIDTH // SSM_HEAD_DIM
SSM_GROUPS = 8
SSM_STATE = 128
SSM_GROUP_WIDTH = SSM_WIDTH // SSM_GROUPS
SSM_GROUP_HEADS = SSM_HEADS // SSM_GROUPS
MIX_WIDTH = ML_WIDTH + GDN_WIDTH + SSM_WIDTH
DEEPNORM_ALPHA = (2 * DEPTH) ** 0.25
RMS_EPS = 1e-6
LN_EPS = 1e-5
NEG_BIG = -1e30

ML_MAIN = ML_HEADS * ML_DK * 2 + 3 * ML_WIDTH
ML_COLS = ML_MAIN + 2 * ML_HEADS
GDN_MAIN = 4 * GDN_WIDTH
GDN_COLS = GDN_MAIN + 2 * GDN_HEADS
SSM_MAIN = 2 * SSM_WIDTH + 2 * SSM_GROUPS * SSM_STATE
SSM_COLS = SSM_MAIN + SSM_HEADS

P_ML_Q = 0
P_ML_K = P_ML_Q + ML_HEADS * ML_DK
P_ML_V = P_ML_K + ML_HEADS * ML_DK
P_ML_O = P_ML_V + ML_WIDTH
P_ML_Z = P_ML_O + ML_WIDTH
P_GDN_Q = 0
P_GDN_K = P_GDN_Q + GDN_WIDTH
P_GDN_V = P_GDN_K + GDN_WIDTH
P_GDN_Z = P_GDN_V + GDN_WIDTH
P_SSM_Z = 0
P_SSM_X = P_SSM_Z + SSM_WIDTH
P_SSM_B = P_SSM_X + SSM_WIDTH
P_SSM_C = P_SSM_B + SSM_GROUPS * SSM_STATE

GATE_COLS = 128
G_ML_I = 0
G_ML_F = G_ML_I + ML_HEADS
G_GDN_BETA = G_ML_F + ML_HEADS
G_GDN_A = G_GDN_BETA + GDN_HEADS
G_SSM_DT = G_GDN_A + GDN_HEADS
G_END = G_SSM_DT + SSM_HEADS

HALO = 8
SEQ_BLOCK = 512
VMEM_LIMIT = 56 * 1024 * 1024


def _sigmoid(u):
    return 1.0 / (1.0 + jnp.exp(-u))


def _softplus(u):
    return jnp.maximum(u, 0.0) + jnp.log1p(jnp.exp(-jnp.abs(u)))


def _dot(a, b):
    return jnp.dot(a.astype(BF16), b.astype(BF16), preferred_element_type=F32)


def _dot_nt(a, b):
    return lax.dot_general(a.astype(BF16), b.astype(BF16), (((1,), (1,)), ((), ())),
                           preferred_element_type=F32)


def _dot_tn(a, b):
    return lax.dot_general(a.astype(BF16), b.astype(BF16), (((0,), (0,)), ((), ())),
                           preferred_element_type=F32)


def _rms_rows(u, w_row):
    return u * lax.rsqrt(jnp.mean(u * u, axis=-1, keepdims=True) + RMS_EPS) * w_row


def _chunk_masks():
    ri = lax.broadcasted_iota(jnp.int32, (CHUNK, CHUNK), 0)
    ci = lax.broadcasted_iota(jnp.int32, (CHUNK, CHUNK), 1)
    return ci <= ri, ci < ri, ci == ri, ri <= ci


def _gate_column(g, col):
    lane = lax.broadcasted_iota(jnp.int32, g.shape, 1)
    return jnp.sum(jnp.where(lane == col, g, 0.0), axis=1, keepdims=True)


def _mm_kernel(x_ref, w_ref, o_ref, *scratch, nk):
    part = jnp.dot(x_ref[...], w_ref[...], preferred_element_type=F32)
    if nk == 1:
        o_ref[...] = part.astype(o_ref.dtype)
        return
    acc_ref, = scratch
    k = pl.program_id(2)

    @pl.when(k == 0)
    def _():
        acc_ref[...] = part

    @pl.when(k > 0)
    def _():
        acc_ref[...] += part

    @pl.when(k == nk - 1)
    def _():
        o_ref[...] = acc_ref[...].astype(o_ref.dtype)


def _matmul(x, w, out_dtype, tm, tn, tk):
    m, kdim = x.shape
    n = w.shape[1]
    tm, tn, tk = min(tm, m), min(tn, n), min(tk, kdim)
    nk = kdim // tk
    scratch = [] if nk == 1 else [pltpu.VMEM((tm, tn), F32)]
    return pl.pallas_call(
        functools.partial(_mm_kernel, nk=nk),
        grid=(n // tn, m // tm, nk),
        in_specs=[pl.BlockSpec((tm, tk), lambda j, i, k: (i, k)),
                  pl.BlockSpec((tk, tn), lambda j, i, k: (k, j))],
        out_specs=pl.BlockSpec((tm, tn), lambda j, i, k: (i, j)),
        out_shape=jax.ShapeDtypeStruct((m, n), out_dtype),
        scratch_shapes=scratch,
        compiler_params=pltpu.CompilerParams(
            dimension_semantics=("parallel", "parallel", "arbitrary"),
            vmem_limit_bytes=VMEM_LIMIT),
        name="proj_matmul",
    )(x, w)


def _out_mm_kernel(yml_ref, ygdn_ref, yssm_ref, w_ref, o_ref, acc_ref, *, nk):
    k = pl.program_id(2)

    @pl.when(k == 0)
    def _():
        acc_ref[...] = jnp.dot(yml_ref[...], w_ref[...], preferred_element_type=F32)

    @pl.when(k == 1)
    def _():
        acc_ref[...] += jnp.dot(ygdn_ref[...], w_ref[...], preferred_element_type=F32)

    @pl.when(jnp.logical_and(k >= 2, k < nk - 1))
    def _():
        acc_ref[...] += jnp.dot(yssm_ref[...], w_ref[...], preferred_element_type=F32)

    @pl.when(k == nk - 1)
    def _():
        o_ref[...] = acc_ref[...] + jnp.dot(yssm_ref[...], w_ref[...], preferred_element_type=F32)


def _out_matmul(y_ml, y_gdn, y_ssm, w, tm=1024, tn=1024):
    m = y_ml.shape[0]
    n = w.shape[1]
    tk = ML_WIDTH
    assert y_ml.shape[1] == tk and y_gdn.shape[1] == tk and y_ssm.shape[1] % tk == 0
    nk = w.shape[0] // tk
    tm, tn = min(tm, m), min(tn, n)
    return pl.pallas_call(
        functools.partial(_out_mm_kernel, nk=nk),
        grid=(n // tn, m // tm, nk),
        in_specs=[pl.BlockSpec((tm, tk), lambda j, i, k: (i, 0)),
                  pl.BlockSpec((tm, tk), lambda j, i, k: (i, 0)),
                  pl.BlockSpec((tm, tk), lambda j, i, k: (i, jnp.maximum(k - 2, 0))),
                  pl.BlockSpec((tk, tn), lambda j, i, k: (k, j))],
        out_specs=pl.BlockSpec((tm, tn), lambda j, i, k: (i, j)),
        out_shape=jax.ShapeDtypeStruct((m, n), F32),
        scratch_shapes=[pltpu.VMEM((tm, tn), F32)],
        compiler_params=pltpu.CompilerParams(
            dimension_semantics=("parallel", "parallel", "arbitrary"),
            vmem_limit_bytes=VMEM_LIMIT),
        name="out_matmul",
    )(y_ml, y_gdn, y_ssm, w)


def _gates_kernel(x_ref, w_ref, bias_ref, alog_ref, o_ref):
    raw = jnp.dot(x_ref[...], w_ref[...], preferred_element_type=F32, precision=HIGHEST)
    col = lax.broadcasted_iota(jnp.int32, raw.shape, 1)
    u = raw + bias_ref[...]
    sp = _softplus(u)
    out = jnp.where(col < G_ML_F, u,
          jnp.where(col < G_GDN_BETA, u - sp,
          jnp.where(col < G_GDN_A, _sigmoid(u),
          jnp.where(col < G_SSM_DT, -jnp.exp(alog_ref[...]) * sp,
                    sp))))
    o_ref[...] = out


def _gates(x, w_gate, bias_row, alog_row, tm=512):
    m, kdim = x.shape
    tm = min(tm, m)
    return pl.pallas_call(
        _gates_kernel,
        grid=(m // tm,),
        in_specs=[pl.BlockSpec((tm, kdim), lambda i: (i, 0)),
                  pl.BlockSpec((kdim, GATE_COLS), lambda i: (0, 0)),
                  pl.BlockSpec((1, GATE_COLS), lambda i: (0, 0)),
                  pl.BlockSpec((1, GATE_COLS), lambda i: (0, 0))],
        out_specs=pl.BlockSpec((tm, GATE_COLS), lambda i: (i, 0)),
        out_shape=jax.ShapeDtypeStruct((m, GATE_COLS), F32),
        compiler_params=pltpu.CompilerParams(dimension_semantics=("parallel",),
                                             vmem_limit_bytes=VMEM_LIMIT),
        name="gate_proj",
    )(x, w_gate, bias_row, alog_row)


ML_HB = 4


def _mlstm_kernel(q_ref, k_ref, v_ref, o_ref, z_ref, g_ref, nw_ref, out_ref,
                  c_ref, n_ref, m_ref, *, ts):
    head0 = pl.program_id(1) * ML_HB

    @pl.when(pl.program_id(2) == 0)
    def _():
        c_ref[...] = jnp.zeros_like(c_ref)
        n_ref[...] = jnp.zeros_like(n_ref)
        m_ref[...] = jnp.zeros_like(m_ref)

    causal, _, eye, upper = _chunk_masks()

    def chunk(c, carry):
        r0 = pl.multiple_of(c * CHUNK, CHUNK)
        rows = pl.ds(r0, CHUNK)
        g = g_ref[rows, :]
        heads = range(ML_HB)
        kl = [slice(hh * ML_DK, (hh + 1) * ML_DK) for hh in heads]
        vl = [slice(hh * ML_DV, (hh + 1) * ML_DV) for hh in heads]
        qf = [q_ref[rows, kl[hh]] * (ML_DK ** -0.5) for hh in heads]
        q = [qf[hh].astype(BF16) for hh in heads]
        k = [k_ref[rows, kl[hh]] for hh in heads]
        v = [v_ref[rows, vl[hh]].astype(BF16) for hh in heads]
        qk = [_dot_nt(q[hh], k[hh]) for hh in heads]
        q_c = [_dot(q[hh], c_ref[hh]) for hh in heads]
        li_c = [_gate_column(g, G_ML_I + head0 + hh) for hh in heads]
        lf_c = [_gate_column(g, G_ML_F + head0 + hh) for hh in heads]
        li_r = [jnp.sum(jnp.where(eye, li_c[hh], 0.0), axis=0, keepdims=True) for hh in heads]
        lf_r = [jnp.sum(jnp.where(eye, lf_c[hh], 0.0), axis=0, keepdims=True) for hh in heads]
        bcum_c = [jnp.sum(jnp.where(causal, lf_r[hh], 0.0), axis=1, keepdims=True) for hh in heads]
        bcum_r = [jnp.sum(jnp.where(upper, lf_c[hh], 0.0), axis=0, keepdims=True) for hh in heads]
        m_prev = [m_ref[hh] for hh in heads]
        dlog = [jnp.where(causal, bcum_c[hh] - bcum_r[hh] + li_r[hh], NEG_BIG) for hh in heads]
        inter = [bcum_c[hh] + m_prev[hh] for hh in heads]
        m_t = [jnp.maximum(inter[hh], jnp.max(dlog[hh], axis=1, keepdims=True)) for hh in heads]
        w_inter = [jnp.exp(inter[hh] - m_t[hh]) for hh in heads]
        sc = [qk[hh] * jnp.exp(dlog[hh] - m_t[hh]) for hh in heads]
        num = [_dot(sc[hh], v[hh]) + w_inter[hh] * q_c[hh] for hh in heads]
        den = [jnp.sum(sc[hh], axis=1, keepdims=True)
               + w_inter[hh] * jnp.sum(qf[hh] * n_ref[hh], axis=1, keepdims=True)
               for hh in heads]
        hval = [num[hh] / jnp.maximum(jnp.abs(den[hh]), jnp.exp(-m_t[hh])) for hh in heads]
        for hh in heads:
            b_last = bcum_c[hh][CHUNK - 1:CHUNK, :]
            gs_c = b_last - bcum_c[hh] + li_c[hh]
            gs_r = b_last - bcum_r[hh] + li_r[hh]
            m_new = jnp.maximum(b_last + m_prev[hh], jnp.max(gs_r, axis=1, keepdims=True))
            dec = jnp.exp(b_last + m_prev[hh] - m_new)
            kw = k[hh] * jnp.exp(gs_c - m_new)
            c_ref[hh] = dec * c_ref[hh] + _dot_tn(kw, v[hh])
            n_ref[hh] = dec * n_ref[hh] + jnp.sum(kw, axis=0, keepdims=True)
            m_ref[hh] = m_new
        for hh in heads:
            y = _rms_rows(hval[hh], nw_ref[:, vl[hh]])
            zz = z_ref[rows, vl[hh]]
            y = y * _sigmoid(o_ref[rows, vl[hh]]) * (zz * _sigmoid(zz))
            out_ref[rows, vl[hh]] = y.astype(out_ref.dtype)
        return carry

    lax.fori_loop(0, ts // CHUNK, chunk, 0)


def _mlstm(proj, gact, norm_w, batch, seq):
    ts = min(SEQ_BLOCK, seq)
    ns = seq // ts
    kw_, vw = ML_HB * ML_DK, ML_HB * ML_DV
    row = lambda b, h, s: b * ns + s
    qk_blk = lambda off: pl.BlockSpec((ts, kw_), lambda b, h, s: (row(b, h, s), off // kw_ + h))
    v_blk = lambda off: pl.BlockSpec((ts, vw), lambda b, h, s: (row(b, h, s), off // vw + h))
    return pl.pallas_call(
        functools.partial(_mlstm_kernel, ts=ts),
        grid=(batch, ML_HEADS // ML_HB, ns),
        in_specs=[qk_blk(P_ML_Q), qk_blk(P_ML_K), v_blk(P_ML_V), v_blk(P_ML_O), v_blk(P_ML_Z),
                  pl.BlockSpec((ts, GATE_COLS), lambda b, h, s: (row(b, h, s), 0)),
                  pl.BlockSpec((1, vw), lambda b, h, s: (0, h))],
        out_specs=pl.BlockSpec((ts, vw), lambda b, h, s: (row(b, h, s), h)),
        out_shape=jax.ShapeDtypeStruct((batch * seq, ML_WIDTH), BF16),
        scratch_shapes=[pltpu.VMEM((ML_HB, ML_DK, ML_DV), F32), pltpu.VMEM((ML_HB, 1, ML_DK), F32),
                        pltpu.VMEM((ML_HB, 1, 1), F32)],
        compiler_params=pltpu.CompilerParams(
            dimension_semantics=("parallel", "parallel", "arbitrary"),
            vmem_limit_bytes=VMEM_LIMIT),
        name="mlstm_mixer",
    )(proj, proj, proj, proj, proj, gact, norm_w.reshape(1, ML_WIDTH))


def _stage_rows(cur_ref, halo_ref, buf_ref, first, ts):
    buf_ref[0:HALO, :] = jnp.where(first, 0.0, halo_ref[...])
    buf_ref[HALO:HALO + ts, :] = cur_ref[...]


def _conv_chunk(buf_ref, w_ref, r0, lanes):
    win = buf_ref[pl.ds(r0, HALO + CHUNK), lanes]
    acc = None
    for tap in range(CONV_K):
        lo = HALO - (CONV_K - 1) + tap
        term = w_ref[tap:tap + 1, lanes] * win[lo:lo + CHUNK, :]
        acc = term if acc is None else acc + term
    return acc


GDN_HB = 8


def _gdn_kernel(q_ref, k_ref, v_ref, qh_ref, kh_ref, vh_ref, z_ref, g_ref,
                wq_ref, wk_ref, wv_ref, nw_ref, out_ref,
                s_ref, bq_ref, bk_ref, bv_ref, *, ts):
    head0 = pl.program_id(1) * GDN_HB
    first = pl.program_id(2) == 0
    d = GDN_HEAD_DIM

    @pl.when(first)
    def _():
        s_ref[...] = jnp.zeros_like(s_ref)

    _stage_rows(q_ref, qh_ref, bq_ref, first, ts)
    _stage_rows(k_ref, kh_ref, bk_ref, first, ts)
    _stage_rows(v_ref, vh_ref, bv_ref, first, ts)

    def l2n(u):
        return u * lax.rsqrt(jnp.sum(u * u, axis=-1, keepdims=True) + RMS_EPS)

    def silu(u):
        return u * _sigmoid(u)

    causal, strict, eye, upper = _chunk_masks()
    eye_f = eye.astype(F32)
    nw = nw_ref[...]

    def chunk(c, carry):
        r0 = pl.multiple_of(c * CHUNK, CHUNK)
        rows = pl.ds(r0, CHUNK)
        g = g_ref[rows, :]
        heads = range(GDN_HB)
        lanes = [slice(hh * d, (hh + 1) * d) for hh in heads]
        k = [l2n(silu(_conv_chunk(bk_ref, wk_ref, r0, lanes[hh]))) for hh in heads]
        q = [l2n(silu(_conv_chunk(bq_ref, wq_ref, r0, lanes[hh]))) * (d ** -0.5) for hh in heads]
        kq = [jnp.concatenate([k[hh], q[hh]], axis=0).astype(BF16) for hh in heads]
        kk_qk = [_dot_nt(kq[hh], kq[hh][:CHUNK]) for hh in heads]
        kq_s = [_dot(kq[hh], s_ref[hh]) for hh in heads]
        b_c = [_gate_column(g, G_GDN_BETA + head0 + hh) for hh in heads]
        g_c = [_gate_column(g, G_GDN_A + head0 + hh) for hh in heads]
        g_r = [jnp.sum(jnp.where(eye, g_c[hh], 0.0), axis=0, keepdims=True) for hh in heads]
        gc_c = [jnp.sum(jnp.where(causal, g_r[hh], 0.0), axis=1, keepdims=True) for hh in heads]
        gc_r = [jnp.sum(jnp.where(upper, g_c[hh], 0.0), axis=0, keepdims=True) for hh in heads]
        decay = [jnp.exp(jnp.where(causal, gc_c[hh] - gc_r[hh], NEG_BIG)) for hh in heads]
        a = [jnp.where(strict, b_c[hh] * kk_qk[hh][:CHUNK] * decay[hh], 0.0) for hh in heads]
        tinv = [eye_f - a[hh] for hh in heads]
        apow = a
        for step in range(5):
            apow = [_dot(apow[hh], apow[hh]) for hh in heads]
            tinv = [tinv[hh] + _dot(tinv[hh], apow[hh]) for hh in heads]
        v = [silu(_conv_chunk(bv_ref, wv_ref, r0, lanes[hh])) for hh in heads]
        eg_c = [jnp.exp(gc_c[hh]) for hh in heads]
        v_new = [_dot(tinv[hh], b_c[hh] * (v[hh] - eg_c[hh] * kq_s[hh][:CHUNK])) for hh in heads]
        o = [eg_c[hh] * kq_s[hh][CHUNK:] + _dot(kk_qk[hh][CHUNK:] * decay[hh], v_new[hh]) for hh in heads]
        for hh in heads:
            g_last = gc_c[hh][CHUNK - 1:CHUNK, :]
            s_ref[hh] = (s_ref[hh] * jnp.exp(g_last)
                         + _dot_tn(k[hh] * jnp.exp(g_last - gc_c[hh]), v_new[hh]))
        for hh in heads:
            zz = z_ref[rows, lanes[hh]]
            y = _rms_rows(o[hh], nw) * (zz * _sigmoid(zz))
            out_ref[rows, lanes[hh]] = y.astype(out_ref.dtype)
        return carry

    lax.fori_loop(0, ts // CHUNK, chunk, 0)


def _halo_row(b, s, ns, ts):
    return jnp.maximum((b * ns + s) * (ts // HALO) - 1, 0)


def _gdn(proj, gact, conv_w, norm_w, batch, seq):
    ts = min(SEQ_BLOCK, seq)
    ns = seq // ts
    d = GDN_HEAD_DIM
    w = GDN_HB * d
    row = lambda b, h, s: b * ns + s
    cur = lambda off: pl.BlockSpec((ts, w), lambda b, h, s: (row(b, h, s), off // w + h))
    halo = lambda off: pl.BlockSpec((HALO, w), lambda b, h, s: (_halo_row(b, s, ns, ts), off // w + h))
    cw = lambda off: pl.BlockSpec((CONV_K, w), lambda b, h, s: (0, off // w + h))
    return pl.pallas_call(
        functools.partial(_gdn_kernel, ts=ts),
        grid=(batch, GDN_HEADS // GDN_HB, ns),
        in_specs=[cur(P_GDN_Q), cur(P_GDN_K), cur(P_GDN_V),
                  halo(P_GDN_Q), halo(P_GDN_K), halo(P_GDN_V),
                  cur(P_GDN_Z),
                  pl.BlockSpec((ts, GATE_COLS), lambda b, h, s: (row(b, h, s), 0)),
                  cw(0), cw(GDN_WIDTH), cw(2 * GDN_WIDTH),
                  pl.BlockSpec((1, d), lambda b, h, s: (0, 0))],
        out_specs=pl.BlockSpec((ts, w), lambda b, h, s: (row(b, h, s), h)),
        out_shape=jax.ShapeDtypeStruct((batch * seq, GDN_WIDTH), BF16),
        scratch_shapes=[pltpu.VMEM((GDN_HB, d, d), F32)] + [pltpu.VMEM((HALO + ts, w), F32)] * 3,
        compiler_params=pltpu.CompilerParams(
            dimension_semantics=("parallel", "parallel", "arbitrary"),
            vmem_limit_bytes=VMEM_LIMIT),
        name="gdn_mixer",
    )(proj, proj, proj, proj, proj, proj, proj, gact, conv_w, conv_w, conv_w, norm_w.reshape(1, d))


SSD_GB = 2


def _ssd_kernel(z_ref, x_ref, b_ref, c_ref, xh_ref, bh_ref, ch_ref, g_ref,
                wx_ref, wb_ref, wc_ref, bx_ref, bb_ref, bc_ref, alog_ref, dskip_ref, nw_ref, out_ref,
                st_ref, bufx_ref, bufb_ref, bufc_ref, sel_ref, *, ts):
    group0 = pl.program_id(1) * SSD_GB
    first = pl.program_id(2) == 0
    gw = SSM_GROUP_WIDTH
    hd = SSM_HEAD_DIM
    n = SSM_STATE
    pairs = gw // (2 * hd)

    @pl.when(first)
    def _():
        st_ref[...] = jnp.zeros_like(st_ref)

    _stage_rows(x_ref, xh_ref, bufx_ref, first, ts)
    _stage_rows(b_ref, bh_ref, bufb_ref, first, ts)
    _stage_rows(c_ref, ch_ref, bufc_ref, first, ts)

    sel_r = lax.broadcasted_iota(jnp.int32, (GATE_COLS, gw), 0)
    sel_c = lax.broadcasted_iota(jnp.int32, (GATE_COLS, gw), 1)
    for gg in range(SSD_GB):
        sel_ref[gg] = (sel_r == G_SSM_DT + (group0 + gg) * SSM_GROUP_HEADS + sel_c // hd).astype(F32)

    ri = lax.broadcasted_iota(jnp.int32, (CHUNK, gw), 0)
    cm = lax.broadcasted_iota(jnp.int32, (CHUNK, gw), 1) % hd
    causal_w = cm <= ri
    upper_w = ri <= cm
    tri_f = _chunk_masks()[0].astype(F32)
    lane_lo = lax.broadcasted_iota(jnp.int32, (CHUNK, 2 * hd), 1) < hd

    def silu(u):
        return u * _sigmoid(u)

    def chunk(c, carry):
        r0 = pl.multiple_of(c * CHUNK, CHUNK)
        rows = pl.ds(r0, CHUNK)
        g = g_ref[rows, :]
        groups = range(SSD_GB)
        xl = [slice(gg * gw, (gg + 1) * gw) for gg in groups]
        nl = [slice(gg * n, (gg + 1) * n) for gg in groups]
        dt = [jnp.dot(g, sel_ref[gg], preferred_element_type=F32, precision=HIGHEST) for gg in groups]
        a = [dt[gg] * -jnp.exp(alog_ref[:, xl[gg]]) for gg in groups]
        acs = [jnp.dot(tri_f, a[gg], preferred_element_type=F32, precision=HIGHEST) for gg in groups]
        bm = [silu(_conv_chunk(bufb_ref, wb_ref, r0, nl[gg]) + bb_ref[:, nl[gg]]) for gg in groups]
        cmat = [silu(_conv_chunk(bufc_ref, wc_ref, r0, nl[gg]) + bc_ref[:, nl[gg]]) for gg in groups]
        cb2 = [_dot_nt(cmat[gg], jnp.concatenate([bm[gg], bm[gg]], axis=0)) for gg in groups]
        y_off = [_dot(cmat[gg], st_ref[gg]) for gg in groups]
        x = [silu(_conv_chunk(bufx_ref, wx_ref, r0, xl[gg]) + bx_ref[:, xl[gg]]) for gg in groups]
        xdt = [x[gg] * dt[gg] for gg in groups]
        acs_r = [jnp.sum(jnp.where(upper_w, a[gg], 0.0), axis=0, keepdims=True) for gg in groups]
        lm = [jnp.exp(jnp.where(causal_w, acs[gg] - acs_r[gg], NEG_BIG)) for gg in groups]
        mm = [(jnp.concatenate([cb2[gg]] * pairs, axis=1) * lm[gg]).astype(BF16) for gg in groups]
        y_diag = []
        for gg in groups:
            parts = []
            for p in range(pairs):
                xp = xdt[gg][:, p * 2 * hd:(p + 1) * 2 * hd]
                rhs = jnp.concatenate([jnp.where(lane_lo, xp, 0.0), jnp.where(lane_lo, 0.0, xp)], axis=0)
                parts.append(_dot(mm[gg][:, p * 2 * hd:(p + 1) * 2 * hd], rhs))
            y_diag.append(jnp.concatenate(parts, axis=1))
        for gg in groups:
            a_last = acs[gg][CHUNK - 1:CHUNK, :]
            st_ref[gg] = (st_ref[gg] * jnp.exp(a_last)
                          + _dot_tn(bm[gg], xdt[gg] * jnp.exp(a_last - acs[gg])))
        for gg in groups:
            y = y_diag[gg] + y_off[gg] * jnp.exp(acs[gg]) + dskip_ref[:, xl[gg]] * x[gg]
            zz = z_ref[rows, xl[gg]]
            y = y * (zz * _sigmoid(zz))
            out_ref[rows, xl[gg]] = _rms_rows(y, nw_ref[:, xl[gg]]).astype(out_ref.dtype)
        return carry

    lax.fori_loop(0, ts // CHUNK, chunk, 0)


def _ssd(proj, gact, conv_w, conv_b, a_log, dskip, norm_w, batch, seq):
    ts = min(SEQ_BLOCK, seq)
    ns = seq // ts
    gw, n = SSD_GB * SSM_GROUP_WIDTH, SSD_GB * SSM_STATE
    row = lambda b, g, s: b * ns + s
    cur = lambda off, w: pl.BlockSpec((ts, w), lambda b, g, s: (row(b, g, s), off // w + g))
    halo = lambda off, w: pl.BlockSpec((HALO, w), lambda b, g, s: (_halo_row(b, s, ns, ts), off // w + g))
    par = lambda rows_, off, w: pl.BlockSpec((rows_, w), lambda b, g, s: (0, off // w + g))
    expand = lambda p: jnp.repeat(p, SSM_HEAD_DIM).reshape(1, SSM_WIDTH)
    conv_b = conv_b.reshape(1, -1)
    return pl.pallas_call(
        functools.partial(_ssd_kernel, ts=ts),
        grid=(batch, SSM_GROUPS // SSD_GB, ns),
        in_specs=[cur(P_SSM_Z, gw), cur(P_SSM_X, gw), cur(P_SSM_B, n), cur(P_SSM_C, n),
                  halo(P_SSM_X, gw), halo(P_SSM_B, n), halo(P_SSM_C, n),
                  pl.BlockSpec((ts, GATE_COLS), lambda b, g, s: (row(b, g, s), 0)),
                  par(CONV_K, 0, gw), par(CONV_K, P_SSM_B - P_SSM_X, n), par(CONV_K, P_SSM_C - P_SSM_X, n),
                  par(1, 0, gw), par(1, P_SSM_B - P_SSM_X, n), par(1, P_SSM_C - P_SSM_X, n),
                  par(1, 0, gw), par(1, 0, gw), par(1, 0, gw)],
        out_specs=pl.BlockSpec((ts, gw), lambda b, g, s: (row(b, g, s), g)),
        out_shape=jax.ShapeDtypeStruct((batch * seq, SSM_WIDTH), BF16),
        scratch_shapes=[pltpu.VMEM((SSD_GB, SSM_STATE, SSM_GROUP_WIDTH), F32),
                        pltpu.VMEM((HALO + ts, gw), F32), pltpu.VMEM((HALO + ts, n), F32),
                        pltpu.VMEM((HALO + ts, n), F32),
                        pltpu.VMEM((SSD_GB, GATE_COLS, SSM_GROUP_WIDTH), F32)],
        compiler_params=pltpu.CompilerParams(
            dimension_semantics=("parallel", "parallel", "arbitrary"),
            vmem_limit_bytes=VMEM_LIMIT),
        name="ssd_mixer",
    )(proj, proj, proj, proj, proj, proj, proj, gact, conv_w, conv_w, conv_w, conv_b, conv_b, conv_b,
      expand(a_log), expand(dskip), norm_w.reshape(1, SSM_WIDTH))


def _deepnorm_kernel(y_ref, x_ref, g_ref, b_ref, o_ref, obf_ref):
    u = DEEPNORM_ALPHA * x_ref[...] + y_ref[...]
    mu = jnp.mean(u, axis=-1, keepdims=True)
    var = jnp.mean(jnp.square(u - mu), axis=-1, keepdims=True)
    out = (u - mu) * lax.rsqrt(var + LN_EPS) * g_ref[...] + b_ref[...]
    o_ref[...] = out
    obf_ref[...] = out.astype(BF16)


def _deepnorm(y, x, ln_g, ln_b, tm=256):
    m, d = x.shape
    tm = min(tm, m)
    blk = pl.BlockSpec((tm, d), lambda i: (i, 0))
    vec = pl.BlockSpec((1, d), lambda i: (0, 0))
    return pl.pallas_call(
        _deepnorm_kernel,
        grid=(m // tm,),
        in_specs=[blk, blk, vec, vec],
        out_specs=[blk, blk],
        out_shape=[jax.ShapeDtypeStruct((m, d), F32), jax.ShapeDtypeStruct((m, d), BF16)],
        compiler_params=pltpu.CompilerParams(dimension_semantics=("parallel",),
                                             vmem_limit_bytes=VMEM_LIMIT),
        name="deepnorm_ln",
    )(y, x, ln_g.reshape(1, d), ln_b.reshape(1, d))


def _pack_in_weights(w):
    gdn0 = ML_COLS
    ssm0 = ML_COLS + GDN_COLS
    wide = (w[:, :ML_MAIN].astype(BF16), w[:, gdn0:gdn0 + GDN_MAIN].astype(BF16),
            w[:, ssm0:ssm0 + SSM_MAIN].astype(BF16))
    gate = jnp.concatenate([w[:, ML_MAIN:ML_COLS], w[:, gdn0 + GDN_MAIN:ssm0], w[:, ssm0 + SSM_MAIN:],
                            jnp.zeros((w.shape[0], GATE_COLS - G_END), w.dtype)], axis=1)
    return wide, gate


def _gate_rows(ml_i_bias, ml_f_bias, gdn_dt_bias, gdn_a_log, ssm_dt_bias):
    zeros = lambda n: jnp.zeros((n,), F32)
    bias = jnp.concatenate([ml_i_bias, ml_f_bias, zeros(GDN_HEADS), gdn_dt_bias, ssm_dt_bias,
                            zeros(GATE_COLS - G_END)])
    alog = jnp.concatenate([zeros(G_GDN_A), gdn_a_log, zeros(GATE_COLS - G_SSM_DT)])
    return bias.reshape(1, GATE_COLS), alog.reshape(1, GATE_COLS)


def kernel(x, w_in, w_out, ml_i_bias, ml_f_bias, ml_norm_w, gdn_conv_w, gdn_A_log, gdn_dt_bias,
           gdn_norm_w, ssm_conv_w, ssm_conv_b, ssm_A_log, ssm_dt_bias, ssm_D, ssm_norm_w, ln_g, ln_b):
    batch, seq, d = x.shape
    xf = x.reshape(batch * seq, d).astype(F32)
    xb = xf.astype(BF16)
    for l in range(w_in.shape[0]):
        (w_ml, w_gdn, w_ssm), w_gate = _pack_in_weights(w_in[l])
        bias_row, alog_row = _gate_rows(ml_i_bias[l], ml_f_bias[l], gdn_dt_bias[l], gdn_A_log[l],
                                        ssm_dt_bias[l])
        gact = _gates(xf, w_gate, bias_row, alog_row)
        y_ml = _mlstm(_matmul(xb, w_ml, F32, 1024, 1024, d), gact, ml_norm_w[l], batch, seq)
        y_gdn = _gdn(_matmul(xb, w_gdn, F32, 1024, 1024, d), gact, gdn_conv_w[l], gdn_norm_w[l],
                     batch, seq)
        y_ssm = _ssd(_matmul(xb, w_ssm, F32, 1024, 1024, d), gact, ssm_conv_w[l], ssm_conv_b[l],
                     ssm_A_log[l], ssm_D[l], ssm_norm_w[l], batch, seq)
        y = _out_matmul(y_ml, y_gdn, y_ssm, w_out[l].astype(BF16))
        xf, xb = _deepnorm(y, xf, ln_g[l], ln_b[l])
    return xf.reshape(batch, seq, d).astype(x.dtype)
```

```python
import functools

import jax
import jax.numpy as jnp
from jax import lax
from jax.experimental import pallas as pl
from jax.experimental.pallas import tpu as pltpu

F32 = jnp.float32
BF16 = jnp.bfloat16
HIGHEST = lax.Precision.HIGHEST

D_MODEL = 4096
DEPTH = 2
CHUNK = 64
CONV_K = 4
ML_WIDTH = D_MODEL // 2
ML_HEADS = 4
ML_DV = ML_WIDTH // ML_HEADS
ML_DK = ML_DV // 2
GDN_WIDTH = D_MODEL // 2
GDN_HEAD_DIM = 128
GDN_HEADS = GDN_WIDTH // GDN_HEAD_DIM
SSM_WIDTH = D_MODEL
SSM_HEAD_DIM = 64
SSM_HEADS = SSM_WIDTH // SSM_HEAD_DIM
SSM_GROUPS = 8
SSM_STATE = 128
SSM_GROUP_WIDTH = SSM_WIDTH // SSM_GROUPS
SSM_GROUP_HEADS = SSM_HEADS // SSM_GROUPS
MIX_WIDTH = ML_WIDTH + GDN_WIDTH + SSM_WIDTH
DEEPNORM_ALPHA = (2 * DEPTH) ** 0.25
RMS_EPS = 1e-6
LN_EPS = 1e-5
NEG_BIG = -1e30

ML_MAIN = ML_HEADS * ML_DK * 2 + 3 * ML_WIDTH
ML_COLS = ML_MAIN + 2 * ML_HEADS
GDN_MAIN = 4 * GDN_WIDTH
GDN_COLS = GDN_MAIN + 2 * GDN_HEADS
SSM_MAIN = 2 * SSM_WIDTH + 2 * SSM_GROUPS * SSM_STATE
SSM_COLS = SSM_MAIN + SSM_HEADS

P_ML_Q = 0
P_ML_K = P_ML_Q + ML_HEADS * ML_DK
P_ML_V = P_ML_K + ML_HEADS * ML_DK
P_ML_O = P_ML_V + ML_WIDTH
P_ML_Z = P_ML_O + ML_WIDTH
P_GDN_Q = 0
P_GDN_K = P_GDN_Q + GDN_WIDTH
P_GDN_V = P_GDN_K + GDN_WIDTH
P_GDN_Z = P_GDN_V + GDN_WIDTH
P_SSM_Z = 0
P_SSM_X = P_SSM_Z + SSM_WIDTH
P_SSM_B = P_SSM_X + SSM_WIDTH
P_SSM_C = P_SSM_B + SSM_GROUPS * SSM_STATE

GATE_COLS = 128
G_ML_I = 0
G_ML_F = G_ML_I + ML_HEADS
G_GDN_BETA = G_ML_F + ML_HEADS
G_GDN_A = G_GDN_BETA + GDN_HEADS
G_SSM_DT = G_GDN_A + GDN_HEADS
G_END = G_SSM_DT + SSM_HEADS

HALO = 8
SEQ_BLOCK = 512
VMEM_LIMIT = 56 * 1024 * 1024


def _sigmoid(u):
    return 1.0 / (1.0 + jnp.exp(-u))


def _softplus(u):
    return jnp.maximum(u, 0.0) + jnp.log1p(jnp.exp(-jnp.abs(u)))


def _dot(a, b):
    return jnp.dot(a.astype(BF16), b.astype(BF16), preferred_element_type=F32)


def _dot_nt(a, b):
    return lax.dot_general(a.astype(BF16), b.astype(BF16), (((1,), (1,)), ((), ())),
                           preferred_element_type=F32)


def _dot_tn(a, b):
    return lax.dot_general(a.astype(BF16), b.astype(BF16), (((0,), (0,)), ((), ())),
                           preferred_element_type=F32)


def _split3(u):
    hi = u.astype(BF16)
    rest = u - hi.astype(F32)
    mid = rest.astype(BF16)
    lo = (rest - mid.astype(F32)).astype(BF16)
    return jnp.concatenate([hi, mid, lo], axis=1)


def _rms_rows(u, w_row):
    return u * lax.rsqrt(jnp.mean(u * u, axis=-1, keepdims=True) + RMS_EPS) * w_row


def _chunk_masks():
    ri = lax.broadcasted_iota(jnp.int32, (CHUNK, CHUNK), 0)
    ci = lax.broadcasted_iota(jnp.int32, (CHUNK, CHUNK), 1)
    return ci <= ri, ci < ri, ci == ri, ri <= ci


def _gate_column(g, col):
    lane = lax.broadcasted_iota(jnp.int32, g.shape, 1)
    return jnp.sum(jnp.where(lane == col, g, 0.0), axis=1, keepdims=True)


LANES = 128
PACK_ROWS = 1024
PACK_COLS = 512
PACK_SUB = 256


def _cast_kernel(w_ref, o_ref):
    def body(r, carry):
        rows = pl.ds(pl.multiple_of(r * PACK_SUB, PACK_SUB), PACK_SUB)
        o_ref[rows, :] = w_ref[rows, :].astype(BF16)
        return carry

    lax.fori_loop(0, PACK_ROWS // PACK_SUB, body, 0)


def _cast_bf16(w):
    layers, rows, cols = w.shape
    assert rows % PACK_ROWS == 0 and cols % PACK_COLS == 0
    blk = pl.BlockSpec((None, PACK_ROWS, PACK_COLS), lambda l, r, j: (l, r, j))
    return pl.pallas_call(
        _cast_kernel,
        grid=(layers, rows // PACK_ROWS, cols // PACK_COLS),
        in_specs=[blk],
        out_specs=blk,
        out_shape=jax.ShapeDtypeStruct((layers, rows, cols), BF16),
        compiler_params=pltpu.CompilerParams(
            dimension_semantics=("parallel", "parallel", "parallel"), vmem_limit_bytes=VMEM_LIMIT),
        name="cast_weights",
    )(w)


PROJ_TM = 1024
PROJ_TN = 512


def _proj_kernel(x_ref, wt_ref, o_ref, wbf_ref):
    @pl.when(pl.program_id(1) == 0)
    def _():
        def body(r, carry):
            rows = pl.ds(pl.multiple_of(r * LANES, LANES), LANES)
            wbf_ref[rows, :] = wt_ref[rows, :].astype(BF16)
            return carry

        lax.fori_loop(0, wt_ref.shape[0] // LANES, body, 0)

    o_ref[...] = lax.dot_general(x_ref[...], wbf_ref[...], (((1,), (1,)), ((), ())),
                                 preferred_element_type=F32)


def _proj(x, w_t, layer, start, width):
    m, kdim = x.shape
    tm, tn = min(PROJ_TM, m), PROJ_TN
    assert start % 8 == 0 and width % tn == 0 and m % tm == 0
    return pl.pallas_call(
        _proj_kernel,
        grid=(width // tn, m // tm),
        in_specs=[pl.BlockSpec((tm, kdim), lambda j, i: (i, 0)),
                  pl.BlockSpec((None, pl.Element(tn), pl.Element(kdim)),
                               lambda j, i: (layer, (start // 8 + j * (tn // 8)) * 8, 0))],
        out_specs=pl.BlockSpec((tm, tn), lambda j, i: (i, j)),
        out_shape=jax.ShapeDtypeStruct((m, width), F32),
        scratch_shapes=[pltpu.VMEM((tn, kdim), BF16)],
        compiler_params=pltpu.CompilerParams(
            dimension_semantics=("parallel", "arbitrary"), vmem_limit_bytes=VMEM_LIMIT),
        name="proj_matmul",
    )(x, w_t)


def _out_mm_kernel(yml_ref, ygdn_ref, yssm_ref, w_ref, o_ref, acc_ref, *, nk):
    k = pl.program_id(2)

    @pl.when(k == 0)
    def _():
        acc_ref[...] = jnp.dot(yml_ref[...], w_ref[...], preferred_element_type=F32)

    @pl.when(k == 1)
    def _():
        acc_ref[...] += jnp.dot(ygdn_ref[...], w_ref[...], preferred_element_type=F32)

    @pl.when(jnp.logical_and(k >= 2, k < nk - 1))
    def _():
        acc_ref[...] += jnp.dot(yssm_ref[...], w_ref[...], preferred_element_type=F32)

    @pl.when(k == nk - 1)
    def _():
        o_ref[...] = acc_ref[...] + jnp.dot(yssm_ref[...], w_ref[...], preferred_element_type=F32)


def _out_matmul(y_ml, y_gdn, y_ssm, w, layer, tm=1024, tn=1024):
    m = y_ml.shape[0]
    n = w.shape[2]
    tk = ML_WIDTH
    assert y_ml.shape[1] == tk and y_gdn.shape[1] == tk and y_ssm.shape[1] % tk == 0
    nk = w.shape[1] // tk
    tm, tn = min(tm, m), min(tn, n)
    return pl.pallas_call(
        functools.partial(_out_mm_kernel, nk=nk),
        grid=(n // tn, m // tm, nk),
        in_specs=[pl.BlockSpec((tm, tk), lambda j, i, k: (i, 0)),
                  pl.BlockSpec((tm, tk), lambda j, i, k: (i, 0)),
                  pl.BlockSpec((tm, tk), lambda j, i, k: (i, jnp.maximum(k - 2, 0))),
                  pl.BlockSpec((None, tk, tn), lambda j, i, k: (layer, k, j))],
        out_specs=pl.BlockSpec((tm, tn), lambda j, i, k: (i, j)),
        out_shape=jax.ShapeDtypeStruct((m, n), F32),
        scratch_shapes=[pltpu.VMEM((tm, tn), F32)],
        compiler_params=pltpu.CompilerParams(
            dimension_semantics=("parallel", "parallel", "arbitrary"),
            vmem_limit_bytes=VMEM_LIMIT),
        name="out_matmul",
    )(y_ml, y_gdn, y_ssm, w)


def _gates_kernel(x_ref, wml_ref, wgdn_ref, wssm_ref, bias_ref, alog_ref, o_ref, *maybe_xb_ref):
    pad = jnp.zeros((GATE_COLS - G_END, x_ref.shape[1]), F32)
    w_gate = jnp.concatenate([wml_ref[...], wgdn_ref[...], wssm_ref[...], pad], axis=0)
    x = x_ref[...]
    for xb_ref in maybe_xb_ref:
        xb_ref[...] = x.astype(BF16)
    raw = lax.dot_general(x, w_gate, (((1,), (1,)), ((), ())), preferred_element_type=F32,
                          precision=HIGHEST)
    col = lax.broadcasted_iota(jnp.int32, raw.shape, 1)
    u = raw + bias_ref[...]
    sp = _softplus(u)
    out = jnp.where(col < G_ML_F, u,
          jnp.where(col < G_GDN_BETA, u - sp,
          jnp.where(col < G_GDN_A, _sigmoid(u),
          jnp.where(col < G_SSM_DT, -jnp.exp(alog_ref[...]) * sp,
                    sp))))
    o_ref[...] = out


def _gates(x, w_t, layer, bias_row, alog_row, emit_bf16, tm=512):
    m, kdim = x.shape
    tm = min(tm, m)
    gate_rows = ((ML_MAIN, 2 * ML_HEADS), (ML_COLS + GDN_MAIN, 2 * GDN_HEADS),
                 (ML_COLS + GDN_COLS + SSM_MAIN, SSM_HEADS))
    assert all(s % 8 == 0 and n % 8 == 0 for s, n in gate_rows)
    wblk = lambda s, n: pl.BlockSpec((None, pl.Element(n), pl.Element(kdim)), lambda i: (layer, s, 0))
    row = pl.BlockSpec((1, GATE_COLS), lambda i: (0, 0))
    out_specs = [pl.BlockSpec((tm, GATE_COLS), lambda i: (i, 0))]
    out_shape = [jax.ShapeDtypeStruct((m, GATE_COLS), F32)]
    if emit_bf16:
        out_specs.append(pl.BlockSpec((tm, kdim), lambda i: (i, 0)))
        out_shape.append(jax.ShapeDtypeStruct((m, kdim), BF16))
    return pl.pallas_call(
        _gates_kernel,
        grid=(m // tm,),
        in_specs=[pl.BlockSpec((tm, kdim), lambda i: (i, 0))] + [wblk(s, n) for s, n in gate_rows] + [row, row],
        out_specs=out_specs,
        out_shape=out_shape,
        compiler_params=pltpu.CompilerParams(dimension_semantics=("parallel",),
                                             vmem_limit_bytes=VMEM_LIMIT),
        name="gate_proj",
    )(x, w_t, w_t, w_t, bias_row, alog_row)


ML_HB = 4


def _mlstm_kernel(q_ref, k_ref, v_ref, o_ref, z_ref, g_ref, nw_ref, out_ref,
                  c_ref, n_ref, m_ref, *, ts):
    head0 = pl.program_id(1) * ML_HB

    @pl.when(pl.program_id(2) == 0)
    def _():
        c_ref[...] = jnp.zeros_like(c_ref)
        n_ref[...] = jnp.zeros_like(n_ref)
        m_ref[...] = jnp.zeros_like(m_ref)

    causal, _, eye, upper = _chunk_masks()

    def chunk(c, carry):
        r0 = pl.multiple_of(c * CHUNK, CHUNK)
        rows = pl.ds(r0, CHUNK)
        g = g_ref[rows, :]
        heads = range(ML_HB)
        kl = [slice(hh * ML_DK, (hh + 1) * ML_DK) for hh in heads]
        vl = [slice(hh * ML_DV, (hh + 1) * ML_DV) for hh in heads]
        qf = [q_ref[rows, kl[hh]] * (ML_DK ** -0.5) for hh in heads]
        q = [qf[hh].astype(BF16) for hh in heads]
        k = [k_ref[rows, kl[hh]] for hh in heads]
        v = [v_ref[rows, vl[hh]].astype(BF16) for hh in heads]
        qk = [_dot_nt(q[hh], k[hh]) for hh in heads]
        q_c = [_dot(q[hh], c_ref[hh]) for hh in heads]
        li_c = [_gate_column(g, G_ML_I + head0 + hh) for hh in heads]
        lf_c = [_gate_column(g, G_ML_F + head0 + hh) for hh in heads]
        li_r = [jnp.sum(jnp.where(eye, li_c[hh], 0.0), axis=0, keepdims=True) for hh in heads]
        lf_r = [jnp.sum(jnp.where(eye, lf_c[hh], 0.0), axis=0, keepdims=True) for hh in heads]
        bcum_c = [jnp.sum(jnp.where(causal, lf_r[hh], 0.0), axis=1, keepdims=True) for hh in heads]
        bcum_r = [jnp.sum(jnp.where(upper, lf_c[hh], 0.0), axis=0, keepdims=True) for hh in heads]
        m_prev = [m_ref[hh] for hh in heads]
        dlog = [jnp.where(causal, bcum_c[hh] - bcum_r[hh] + li_r[hh], NEG_BIG) for hh in heads]
        inter = [bcum_c[hh] + m_prev[hh] for hh in heads]
        m_t = [jnp.maximum(inter[hh], jnp.max(dlog[hh], axis=1, keepdims=True)) for hh in heads]
        w_inter = [jnp.exp(inter[hh] - m_t[hh]) for hh in heads]
        sc = [qk[hh] * jnp.exp(dlog[hh] - m_t[hh]) for hh in heads]
        num = [_dot(sc[hh], v[hh]) + w_inter[hh] * q_c[hh] for hh in heads]
        den = [jnp.sum(sc[hh], axis=1, keepdims=True)
               + w_inter[hh] * jnp.sum(qf[hh] * n_ref[hh], axis=1, keepdims=True)
               for hh in heads]
        hval = [num[hh] / jnp.maximum(jnp.abs(den[hh]), jnp.exp(-m_t[hh])) for hh in heads]
        for hh in heads:
            b_last = bcum_c[hh][CHUNK - 1:CHUNK, :]
            gs_c = b_last - bcum_c[hh] + li_c[hh]
            gs_r = b_last - bcum_r[hh] + li_r[hh]
            m_new = jnp.maximum(b_last + m_prev[hh], jnp.max(gs_r, axis=1, keepdims=True))
            dec = jnp.exp(b_last + m_prev[hh] - m_new)
            kw = k[hh] * jnp.exp(gs_c - m_new)
            c_ref[hh] = dec * c_ref[hh] + _dot_tn(kw, v[hh])
            n_ref[hh] = dec * n_ref[hh] + jnp.sum(kw, axis=0, keepdims=True)
            m_ref[hh] = m_new
        for hh in heads:
            y = _rms_rows(hval[hh], nw_ref[:, vl[hh]])
            zz = z_ref[rows, vl[hh]]
            y = y * _sigmoid(o_ref[rows, vl[hh]]) * (zz * _sigmoid(zz))
            out_ref[rows, vl[hh]] = y.astype(out_ref.dtype)
        return carry

    lax.fori_loop(0, ts // CHUNK, chunk, 0)


def _mlstm(proj, gact, norm_w, batch, seq):
    ts = min(SEQ_BLOCK, seq)
    ns = seq // ts
    kw_, vw = ML_HB * ML_DK, ML_HB * ML_DV
    row = lambda b, h, s: b * ns + s
    qk_blk = lambda off: pl.BlockSpec((ts, kw_), lambda b, h, s: (row(b, h, s), off // kw_ + h))
    v_blk = lambda off: pl.BlockSpec((ts, vw), lambda b, h, s: (row(b, h, s), off // vw + h))
    return pl.pallas_call(
        functools.partial(_mlstm_kernel, ts=ts),
        grid=(batch, ML_HEADS // ML_HB, ns),
        in_specs=[qk_blk(P_ML_Q), qk_blk(P_ML_K), v_blk(P_ML_V), v_blk(P_ML_O), v_blk(P_ML_Z),
                  pl.BlockSpec((ts, GATE_COLS), lambda b, h, s: (row(b, h, s), 0)),
                  pl.BlockSpec((1, vw), lambda b, h, s: (0, h))],
        out_specs=pl.BlockSpec((ts, vw), lambda b, h, s: (row(b, h, s), h)),
        out_shape=jax.ShapeDtypeStruct((batch * seq, ML_WIDTH), BF16),
        scratch_shapes=[pltpu.VMEM((ML_HB, ML_DK, ML_DV), F32), pltpu.VMEM((ML_HB, 1, ML_DK), F32),
                        pltpu.VMEM((ML_HB, 1, 1), F32)],
        compiler_params=pltpu.CompilerParams(
            dimension_semantics=("parallel", "parallel", "arbitrary"),
            vmem_limit_bytes=VMEM_LIMIT),
        name="mlstm_mixer",
    )(proj, proj, proj, proj, proj, gact, norm_w.reshape(1, ML_WIDTH))


def _stage_rows(cur_ref, halo_ref, buf_ref, first, ts):
    buf_ref[0:HALO, :] = jnp.where(first, 0.0, halo_ref[...])
    buf_ref[HALO:HALO + ts, :] = cur_ref[...]


def _conv_chunk(buf_ref, w_ref, r0, lanes):
    win = buf_ref[pl.ds(r0, HALO + CHUNK), lanes]
    acc = None
    for tap in range(CONV_K):
        lo = HALO - (CONV_K - 1) + tap
        term = w_ref[tap:tap + 1, lanes] * win[lo:lo + CHUNK, :]
        acc = term if acc is None else acc + term
    return acc


GDN_HB = 8


def _gdn_kernel(q_ref, k_ref, v_ref, qh_ref, kh_ref, vh_ref, z_ref, g_ref,
                wq_ref, wk_ref, wv_ref, nw_ref, out_ref,
                s_ref, bq_ref, bk_ref, bv_ref, *, ts):
    head0 = pl.program_id(1) * GDN_HB
    first = pl.program_id(2) == 0
    d = GDN_HEAD_DIM

    @pl.when(first)
    def _():
        s_ref[...] = jnp.zeros_like(s_ref)

    _stage_rows(q_ref, qh_ref, bq_ref, first, ts)
    _stage_rows(k_ref, kh_ref, bk_ref, first, ts)
    _stage_rows(v_ref, vh_ref, bv_ref, first, ts)

    def l2n(u):
        return u * lax.rsqrt(jnp.sum(u * u, axis=-1, keepdims=True) + RMS_EPS)

    def silu(u):
        return u * _sigmoid(u)

    causal, strict, eye, upper = _chunk_masks()
    eye_f = eye.astype(F32)
    nw = nw_ref[...]

    def chunk(c, carry):
        r0 = pl.multiple_of(c * CHUNK, CHUNK)
        rows = pl.ds(r0, CHUNK)
        g = g_ref[rows, :]
        heads = range(GDN_HB)
        lanes = [slice(hh * d, (hh + 1) * d) for hh in heads]
        k = [l2n(silu(_conv_chunk(bk_ref, wk_ref, r0, lanes[hh]))) for hh in heads]
        q = [l2n(silu(_conv_chunk(bq_ref, wq_ref, r0, lanes[hh]))) * (d ** -0.5) for hh in heads]
        kq = [jnp.concatenate([k[hh], q[hh]], axis=0).astype(BF16) for hh in heads]
        kk_qk = [_dot_nt(kq[hh], kq[hh][:CHUNK]) for hh in heads]
        kq_s = [_dot(kq[hh], s_ref[hh]) for hh in heads]
        b_c = [_gate_column(g, G_GDN_BETA + head0 + hh) for hh in heads]
        g_c = [_gate_column(g, G_GDN_A + head0 + hh) for hh in heads]
        g_r = [jnp.sum(jnp.where(eye, g_c[hh], 0.0), axis=0, keepdims=True) for hh in heads]
        gc_c = [jnp.sum(jnp.where(causal, g_r[hh], 0.0), axis=1, keepdims=True) for hh in heads]
        gc_r = [jnp.sum(jnp.where(upper, g_c[hh], 0.0), axis=0, keepdims=True) for hh in heads]
        decay = [jnp.exp(jnp.where(causal, gc_c[hh] - gc_r[hh], NEG_BIG)) for hh in heads]
        a = [jnp.where(strict, b_c[hh] * kk_qk[hh][:CHUNK] * decay[hh], 0.0) for hh in heads]
        tinv = [eye_f - a[hh] for hh in heads]
        apow = a
        for step in range(5):
            apow = [_dot(apow[hh], apow[hh]) for hh in heads]
            tinv = [tinv[hh] + _dot(tinv[hh], apow[hh]) for hh in heads]
        v = [silu(_conv_chunk(bv_ref, wv_ref, r0, lanes[hh])) for hh in heads]
        eg_c = [jnp.exp(gc_c[hh]) for hh in heads]
        v_new = [_dot(tinv[hh], b_c[hh] * (v[hh] - eg_c[hh] * kq_s[hh][:CHUNK])) for hh in heads]
        o = [eg_c[hh] * kq_s[hh][CHUNK:] + _dot(kk_qk[hh][CHUNK:] * decay[hh], v_new[hh]) for hh in heads]
        for hh in heads:
            g_last = gc_c[hh][CHUNK - 1:CHUNK, :]
            s_ref[hh] = (s_ref[hh] * jnp.exp(g_last)
                         + _dot_tn(k[hh] * jnp.exp(g_last - gc_c[hh]), v_new[hh]))
        for hh in heads:
            zz = z_ref[rows, lanes[hh]]
            y = _rms_rows(o[hh], nw) * (zz * _sigmoid(zz))
            out_ref[rows, lanes[hh]] = y.astype(out_ref.dtype)
        return carry

    lax.fori_loop(0, ts // CHUNK, chunk, 0)


def _halo_row(b, s, ns, ts):
    return jnp.maximum((b * ns + s) * (ts // HALO) - 1, 0)


def _gdn(proj, gact, conv_w, norm_w, batch, seq):
    ts = min(SEQ_BLOCK, seq)
    ns = seq // ts
    d = GDN_HEAD_DIM
    w = GDN_HB * d
    row = lambda b, h, s: b * ns + s
    cur = lambda off: pl.BlockSpec((ts, w), lambda b, h, s: (row(b, h, s), off // w + h))
    halo = lambda off: pl.BlockSpec((HALO, w), lambda b, h, s: (_halo_row(b, s, ns, ts), off // w + h))
    cw = lambda off: pl.BlockSpec((CONV_K, w), lambda b, h, s: (0, off // w + h))
    return pl.pallas_call(
        functools.partial(_gdn_kernel, ts=ts),
        grid=(batch, GDN_HEADS // GDN_HB, ns),
        in_specs=[cur(P_GDN_Q), cur(P_GDN_K), cur(P_GDN_V),
                  halo(P_GDN_Q), halo(P_GDN_K), halo(P_GDN_V),
                  cur(P_GDN_Z),
                  pl.BlockSpec((ts, GATE_COLS), lambda b, h, s: (row(b, h, s), 0)),
                  cw(0), cw(GDN_WIDTH), cw(2 * GDN_WIDTH),
                  pl.BlockSpec((1, d), lambda b, h, s: (0, 0))],
        out_specs=pl.BlockSpec((ts, w), lambda b, h, s: (row(b, h, s), h)),
        out_shape=jax.ShapeDtypeStruct((batch * seq, GDN_WIDTH), BF16),
        scratch_shapes=[pltpu.VMEM((GDN_HB, d, d), F32)] + [pltpu.VMEM((HALO + ts, w), F32)] * 3,
        compiler_params=pltpu.CompilerParams(
            dimension_semantics=("parallel", "parallel", "arbitrary"),
            vmem_limit_bytes=VMEM_LIMIT),
        name="gdn_mixer",
    )(proj, proj, proj, proj, proj, proj, proj, gact, conv_w, conv_w, conv_w, norm_w.reshape(1, d))


SSD_GB = 2


def _ssd_kernel(z_ref, x_ref, b_ref, c_ref, xh_ref, bh_ref, ch_ref, g_ref,
                wx_ref, wb_ref, wc_ref, bx_ref, bb_ref, bc_ref, alog_ref, dskip_ref, nw_ref, out_ref,
                st_ref, bufx_ref, bufb_ref, bufc_ref, sel_ref, *, ts):
    group0 = pl.program_id(1) * SSD_GB
    first = pl.program_id(2) == 0
    gw = SSM_GROUP_WIDTH
    hd = SSM_HEAD_DIM
    n = SSM_STATE
    pairs = gw // (2 * hd)

    @pl.when(first)
    def _():
        st_ref[...] = jnp.zeros_like(st_ref)

    _stage_rows(x_ref, xh_ref, bufx_ref, first, ts)
    _stage_rows(b_ref, bh_ref, bufb_ref, first, ts)
    _stage_rows(c_ref, ch_ref, bufc_ref, first, ts)

    sel_r = lax.broadcasted_iota(jnp.int32, (3 * GATE_COLS, gw), 0) % GATE_COLS
    sel_c = lax.broadcasted_iota(jnp.int32, (3 * GATE_COLS, gw), 1)
    for gg in range(SSD_GB):
        sel_ref[gg] = (sel_r == G_SSM_DT + (group0 + gg) * SSM_GROUP_HEADS + sel_c // hd).astype(BF16)

    ri = lax.broadcasted_iota(jnp.int32, (CHUNK, gw), 0)
    cm = lax.broadcasted_iota(jnp.int32, (CHUNK, gw), 1) % hd
    causal_w = cm <= ri
    upper_w = ri <= cm
    tri_b = _chunk_masks()[0].astype(BF16)
    lane_lo = lax.broadcasted_iota(jnp.int32, (CHUNK, 2 * hd), 1) < hd

    def silu(u):
        return u * _sigmoid(u)

    def chunk(c, carry):
        r0 = pl.multiple_of(c * CHUNK, CHUNK)
        rows = pl.ds(r0, CHUNK)
        g = g_ref[rows, :]
        groups = range(SSD_GB)
        xl = [slice(gg * gw, (gg + 1) * gw) for gg in groups]
        nl = [slice(gg * n, (gg + 1) * n) for gg in groups]
        g3 = _split3(g)
        cum3 = jnp.dot(tri_b, g3, preferred_element_type=F32)
        gcum = cum3[:, :GATE_COLS] + cum3[:, GATE_COLS:2 * GATE_COLS] + cum3[:, 2 * GATE_COLS:]
        both3 = jnp.concatenate([g3, _split3(gcum)], axis=0)
        expanded = [jnp.dot(both3, sel_ref[gg], preferred_element_type=F32) for gg in groups]
        neg_a = [-jnp.exp(alog_ref[:, xl[gg]]) for gg in groups]
        dt = [expanded[gg][:CHUNK] for gg in groups]
        a = [dt[gg] * neg_a[gg] for gg in groups]
        acs = [expanded[gg][CHUNK:] * neg_a[gg] for gg in groups]
        bm = [silu(_conv_chunk(bufb_ref, wb_ref, r0, nl[gg]) + bb_ref[:, nl[gg]]) for gg in groups]
        cmat = [silu(_conv_chunk(bufc_ref, wc_ref, r0, nl[gg]) + bc_ref[:, nl[gg]]) for gg in groups]
        cb2 = [_dot_nt(cmat[gg], jnp.concatenate([bm[gg], bm[gg]], axis=0)) for gg in groups]
        y_off = [_dot(cmat[gg], st_ref[gg]) for gg in groups]
        x = [silu(_conv_chunk(bufx_ref, wx_ref, r0, xl[gg]) + bx_ref[:, xl[gg]]) for gg in groups]
        xdt = [x[gg] * dt[gg] for gg in groups]
        acs_r = [jnp.sum(jnp.where(upper_w, a[gg], 0.0), axis=0, keepdims=True) for gg in groups]
        lm = [jnp.exp(jnp.where(causal_w, acs[gg] - acs_r[gg], NEG_BIG)) for gg in groups]
        mm = [(jnp.concatenate([cb2[gg]] * pairs, axis=1) * lm[gg]).astype(BF16) for gg in groups]
        y_diag = []
        for gg in groups:
            parts = []
            for p in range(pairs):
                xp = xdt[gg][:, p * 2 * hd:(p + 1) * 2 * hd]
                rhs = jnp.concatenate([jnp.where(lane_lo, xp, 0.0), jnp.where(lane_lo, 0.0, xp)], axis=0)
                parts.append(_dot(mm[gg][:, p * 2 * hd:(p + 1) * 2 * hd], rhs))
            y_diag.append(jnp.concatenate(parts, axis=1))
        for gg in groups:
            a_last = acs[gg][CHUNK - 1:CHUNK, :]
            st_ref[gg] = (st_ref[gg] * jnp.exp(a_last)
                          + _dot_tn(bm[gg], xdt[gg] * jnp.exp(a_last - acs[gg])))
        for gg in groups:
            y = y_diag[gg] + y_off[gg] * jnp.exp(acs[gg]) + dskip_ref[:, xl[gg]] * x[gg]
            zz = z_ref[rows, xl[gg]]
            y = y * (zz * _sigmoid(zz))
            out_ref[rows, xl[gg]] = _rms_rows(y, nw_ref[:, xl[gg]]).astype(out_ref.dtype)
        return carry

    lax.fori_loop(0, ts // CHUNK, chunk, 0)


def _ssd(proj, gact, conv_w, conv_b, a_log, dskip, norm_w, batch, seq):
    ts = min(SEQ_BLOCK, seq)
    ns = seq // ts
    gw, n = SSD_GB * SSM_GROUP_WIDTH, SSD_GB * SSM_STATE
    row = lambda b, g, s: b * ns + s
    cur = lambda off, w: pl.BlockSpec((ts, w), lambda b, g, s: (row(b, g, s), off // w + g))
    halo = lambda off, w: pl.BlockSpec((HALO, w), lambda b, g, s: (_halo_row(b, s, ns, ts), off // w + g))
    par = lambda rows_, off, w: pl.BlockSpec((rows_, w), lambda b, g, s: (0, off // w + g))
    expand = lambda p: jnp.repeat(p, SSM_HEAD_DIM).reshape(1, SSM_WIDTH)
    conv_b = conv_b.reshape(1, -1)
    return pl.pallas_call(
        functools.partial(_ssd_kernel, ts=ts),
        grid=(batch, SSM_GROUPS // SSD_GB, ns),
        in_specs=[cur(P_SSM_Z, gw), cur(P_SSM_X, gw), cur(P_SSM_B, n), cur(P_SSM_C, n),
                  halo(P_SSM_X, gw), halo(P_SSM_B, n), halo(P_SSM_C, n),
                  pl.BlockSpec((ts, GATE_COLS), lambda b, g, s: (row(b, g, s), 0)),
                  par(CONV_K, 0, gw), par(CONV_K, P_SSM_B - P_SSM_X, n), par(CONV_K, P_SSM_C - P_SSM_X, n),
                  par(1, 0, gw), par(1, P_SSM_B - P_SSM_X, n), par(1, P_SSM_C - P_SSM_X, n),
                  par(1, 0, gw), par(1, 0, gw), par(1, 0, gw)],
        out_specs=pl.BlockSpec((ts, gw), lambda b, g, s: (row(b, g, s), g)),
        out_shape=jax.ShapeDtypeStruct((batch * seq, SSM_WIDTH), BF16),
        scratch_shapes=[pltpu.VMEM((SSD_GB, SSM_STATE, SSM_GROUP_WIDTH), F32),
                        pltpu.VMEM((HALO + ts, gw), F32), pltpu.VMEM((HALO + ts, n), F32),
                        pltpu.VMEM((HALO + ts, n), F32),
                        pltpu.VMEM((SSD_GB, 3 * GATE_COLS, SSM_GROUP_WIDTH), BF16)],
        compiler_params=pltpu.CompilerParams(
            dimension_semantics=("parallel", "parallel", "arbitrary"),
            vmem_limit_bytes=VMEM_LIMIT),
        name="ssd_mixer",
    )(proj, proj, proj, proj, proj, proj, proj, gact, conv_w, conv_w, conv_w, conv_b, conv_b, conv_b,
      expand(a_log), expand(dskip), norm_w.reshape(1, SSM_WIDTH))


def _deepnorm_kernel(y_ref, x_ref, g_ref, b_ref, o_ref, obf_ref):
    u = DEEPNORM_ALPHA * x_ref[...] + y_ref[...]
    mu = jnp.mean(u, axis=-1, keepdims=True)
    var = jnp.mean(jnp.square(u - mu), axis=-1, keepdims=True)
    out = (u - mu) * lax.rsqrt(var + LN_EPS) * g_ref[...] + b_ref[...]
    o_ref[...] = out
    obf_ref[...] = out.astype(BF16)


def _deepnorm(y, x, ln_g, ln_b, tm=256):
    m, d = x.shape
    tm = min(tm, m)
    blk = pl.BlockSpec((tm, d), lambda i: (i, 0))
    vec = pl.BlockSpec((1, d), lambda i: (0, 0))
    return pl.pallas_call(
        _deepnorm_kernel,
        grid=(m // tm,),
        in_specs=[blk, blk, vec, vec],
        out_specs=[blk, blk],
        out_shape=[jax.ShapeDtypeStruct((m, d), F32), jax.ShapeDtypeStruct((m, d), BF16)],
        compiler_params=pltpu.CompilerParams(dimension_semantics=("parallel",),
                                             vmem_limit_bytes=VMEM_LIMIT),
        name="deepnorm_ln",
    )(y, x, ln_g.reshape(1, d), ln_b.reshape(1, d))


def _gate_rows(ml_i_bias, ml_f_bias, gdn_dt_bias, gdn_a_log, ssm_dt_bias):
    zeros = lambda n: jnp.zeros((n,), F32)
    bias = jnp.concatenate([ml_i_bias, ml_f_bias, zeros(GDN_HEADS), gdn_dt_bias, ssm_dt_bias,
                            zeros(GATE_COLS - G_END)])
    alog = jnp.concatenate([zeros(G_GDN_A), gdn_a_log, zeros(GATE_COLS - G_SSM_DT)])
    return bias.reshape(1, GATE_COLS), alog.reshape(1, GATE_COLS)


def kernel(x, w_in, w_out, ml_i_bias, ml_f_bias, ml_norm_w, gdn_conv_w, gdn_A_log, gdn_dt_bias,
           gdn_norm_w, ssm_conv_w, ssm_conv_b, ssm_A_log, ssm_dt_bias, ssm_D, ssm_norm_w, ln_g, ln_b):
    batch, seq, d = x.shape
    xf = x.reshape(batch * seq, d).astype(F32)
    xb = None
    w_t = jnp.swapaxes(w_in.astype(F32), 1, 2)
    w_o = _cast_bf16(w_out.astype(F32))
    for l in range(w_in.shape[0]):
        bias_row, alog_row = _gate_rows(ml_i_bias[l], ml_f_bias[l], gdn_dt_bias[l], gdn_A_log[l],
                                        ssm_dt_bias[l])
        if xb is None:
            gact, xb = _gates(xf, w_t, l, bias_row, alog_row, True)
        else:
            gact, = _gates(xf, w_t, l, bias_row, alog_row, False)
        y_ml = _mlstm(_proj(xb, w_t, l, 0, ML_MAIN), gact, ml_norm_w[l], batch, seq)
        y_gdn = _gdn(_proj(xb, w_t, l, ML_COLS, GDN_MAIN), gact, gdn_conv_w[l], gdn_norm_w[l],
                     batch, seq)
        y_ssm = _ssd(_proj(xb, w_t, l, ML_COLS + GDN_COLS, SSM_MAIN), gact, ssm_conv_w[l],
                     ssm_conv_b[l], ssm_A_log[l], ssm_D[l], ssm_norm_w[l], batch, seq)
        y = _out_matmul(y_ml, y_gdn, y_ssm, w_o, l)
        xf, xb = _deepnorm(y, xf, ln_g[l], ln_b[l])
    return xf.reshape(batch, seq, d).astype(x.dtype)
```

```python
import functools

import jax
import jax.numpy as jnp
from jax import lax
from jax.experimental import pallas as pl
from jax.experimental.pallas import tpu as pltpu

F32 = jnp.float32
BF16 = jnp.bfloat16
HIGHEST = lax.Precision.HIGHEST

D_MODEL = 4096
DEPTH = 2
CHUNK = 64
CONV_K = 4
ML_WIDTH = D_MODEL // 2
ML_HEADS = 4
ML_DV = ML_WIDTH // ML_HEADS
ML_DK = ML_DV // 2
GDN_WIDTH = D_MODEL // 2
GDN_HEAD_DIM = 128
GDN_HEADS = GDN_WIDTH // GDN_HEAD_DIM
SSM_WIDTH = D_MODEL
SSM_HEAD_DIM = 64
SSM_HEADS = SSM_WIDTH // SSM_HEAD_DIM
SSM_GROUPS = 8
SSM_STATE = 128
SSM_GROUP_WIDTH = SSM_WIDTH // SSM_GROUPS
SSM_GROUP_HEADS = SSM_HEADS // SSM_GROUPS
MIX_WIDTH = ML_WIDTH + GDN_WIDTH + SSM_WIDTH
DEEPNORM_ALPHA = (2 * DEPTH) ** 0.25
RMS_EPS = 1e-6
LN_EPS = 1e-5
NEG_BIG = -1e30

ML_MAIN = ML_HEADS * ML_DK * 2 + 3 * ML_WIDTH
ML_COLS = ML_MAIN + 2 * ML_HEADS
GDN_MAIN = 4 * GDN_WIDTH
GDN_COLS = GDN_MAIN + 2 * GDN_HEADS
SSM_MAIN = 2 * SSM_WIDTH + 2 * SSM_GROUPS * SSM_STATE
SSM_COLS = SSM_MAIN + SSM_HEADS

P_ML_Q = 0
P_ML_K = P_ML_Q + ML_HEADS * ML_DK
P_ML_V = P_ML_K + ML_HEADS * ML_DK
P_ML_O = P_ML_V + ML_WIDTH
P_ML_Z = P_ML_O + ML_WIDTH
P_GDN_Q = 0
P_GDN_K = P_GDN_Q + GDN_WIDTH
P_GDN_V = P_GDN_K + GDN_WIDTH
P_GDN_Z = P_GDN_V + GDN_WIDTH
P_SSM_Z = 0
P_SSM_X = P_SSM_Z + SSM_WIDTH
P_SSM_B = P_SSM_X + SSM_WIDTH
P_SSM_C = P_SSM_B + SSM_GROUPS * SSM_STATE

GATE_COLS = 128
G_ML_I = 0
G_ML_F = G_ML_I + ML_HEADS
G_GDN_BETA = G_ML_F + ML_HEADS
G_GDN_A = G_GDN_BETA + GDN_HEADS
G_SSM_DT = G_GDN_A + GDN_HEADS
G_END = G_SSM_DT + SSM_HEADS

HALO = 8
SEQ_BLOCK = 512
VMEM_LIMIT = 56 * 1024 * 1024


def _sigmoid(u):
    return 0.5 * jnp.tanh(0.5 * u) + 0.5


def _silu(u):
    h = 0.5 * u
    return h * jnp.tanh(h) + h


def _softplus(u):
    return jnp.maximum(u, 0.0) + jnp.log1p(jnp.exp(-jnp.abs(u)))


def _dot(a, b):
    return jnp.dot(a.astype(BF16), b.astype(BF16), preferred_element_type=F32)


def _dot_nt(a, b):
    return lax.dot_general(a.astype(BF16), b.astype(BF16), (((1,), (1,)), ((), ())),
                           preferred_element_type=F32)


def _dot_tn(a, b):
    return lax.dot_general(a.astype(BF16), b.astype(BF16), (((0,), (0,)), ((), ())),
                           preferred_element_type=F32)


def _split3(u):
    hi = u.astype(BF16)
    rest = u - hi.astype(F32)
    mid = rest.astype(BF16)
    lo = (rest - mid.astype(F32)).astype(BF16)
    return jnp.concatenate([hi, mid, lo], axis=1)


def _rms_rows(u, w_row):
    return u * lax.rsqrt(jnp.mean(u * u, axis=-1, keepdims=True) + RMS_EPS) * w_row


def _chunk_masks():
    ri = lax.broadcasted_iota(jnp.int32, (CHUNK, CHUNK), 0)
    ci = lax.broadcasted_iota(jnp.int32, (CHUNK, CHUNK), 1)
    return ci <= ri, ci < ri, ci == ri, ri <= ci


def _gate_column(g, col):
    lane = lax.broadcasted_iota(jnp.int32, g.shape, 1)
    return jnp.sum(jnp.where(lane == col, g, 0.0), axis=1, keepdims=True)


LANES = 128
PACK_ROWS = 1024
PACK_COLS = 512
PACK_SUB = 256


def _cast_kernel(w_ref, o_ref):
    def body(r, carry):
        rows = pl.ds(pl.multiple_of(r * PACK_SUB, PACK_SUB), PACK_SUB)
        o_ref[rows, :] = w_ref[rows, :].astype(BF16)
        return carry

    lax.fori_loop(0, PACK_ROWS // PACK_SUB, body, 0)


def _cast_bf16(w):
    layers, rows, cols = w.shape
    assert rows % PACK_ROWS == 0 and cols % PACK_COLS == 0
    blk = pl.BlockSpec((None, PACK_ROWS, PACK_COLS), lambda l, r, j: (l, r, j))
    return pl.pallas_call(
        _cast_kernel,
        grid=(layers, rows // PACK_ROWS, cols // PACK_COLS),
        in_specs=[blk],
        out_specs=blk,
        out_shape=jax.ShapeDtypeStruct((layers, rows, cols), BF16),
        compiler_params=pltpu.CompilerParams(
            dimension_semantics=("parallel", "parallel", "parallel"), vmem_limit_bytes=VMEM_LIMIT),
        name="cast_weights",
    )(w)


PROJ_TM = 512
PROJ_TN = 1024


def _proj_kernel(x_ref, wt_ref, o_ref, wbf_ref):
    @pl.when(pl.program_id(1) == 0)
    def _():
        def body(r, carry):
            rows = pl.ds(pl.multiple_of(r * LANES, LANES), LANES)
            wbf_ref[rows, :] = wt_ref[rows, :].astype(BF16)
            return carry

        lax.fori_loop(0, wt_ref.shape[0] // LANES, body, 0)

    o_ref[...] = lax.dot_general(x_ref[...], wbf_ref[...], (((1,), (1,)), ((), ())),
                                 preferred_element_type=F32)


def _proj(x, w_t, layer, start, width):
    m, kdim = x.shape
    tm, tn = min(PROJ_TM, m), PROJ_TN
    assert start % 8 == 0 and width % tn == 0 and m % tm == 0
    return pl.pallas_call(
        _proj_kernel,
        grid=(width // tn, m // tm),
        in_specs=[pl.BlockSpec((tm, kdim), lambda j, i: (i, 0)),
                  pl.BlockSpec((None, pl.Element(tn), pl.Element(kdim)),
                               lambda j, i: (layer, (start // 8 + j * (tn // 8)) * 8, 0))],
        out_specs=pl.BlockSpec((tm, tn), lambda j, i: (i, j)),
        out_shape=jax.ShapeDtypeStruct((m, width), F32),
        scratch_shapes=[pltpu.VMEM((tn, kdim), BF16)],
        compiler_params=pltpu.CompilerParams(
            dimension_semantics=("parallel", "arbitrary"), vmem_limit_bytes=VMEM_LIMIT),
        name="proj_matmul",
    )(x, w_t)


def _out_mm_kernel(yml_ref, ygdn_ref, yssm_ref, w_ref, o_ref, acc_ref, *, nk):
    k = pl.program_id(2)

    @pl.when(k == 0)
    def _():
        acc_ref[...] = jnp.dot(yml_ref[...], w_ref[...], preferred_element_type=F32)

    @pl.when(k == 1)
    def _():
        acc_ref[...] += jnp.dot(ygdn_ref[...], w_ref[...], preferred_element_type=F32)

    @pl.when(jnp.logical_and(k >= 2, k < nk - 1))
    def _():
        acc_ref[...] += jnp.dot(yssm_ref[...], w_ref[...], preferred_element_type=F32)

    @pl.when(k == nk - 1)
    def _():
        o_ref[...] = acc_ref[...] + jnp.dot(yssm_ref[...], w_ref[...], preferred_element_type=F32)


def _out_matmul(y_ml, y_gdn, y_ssm, w, layer, tm=1024, tn=1024):
    m = y_ml.shape[0]
    n = w.shape[2]
    tk = ML_WIDTH
    assert y_ml.shape[1] == tk and y_gdn.shape[1] == tk and y_ssm.shape[1] % tk == 0
    nk = w.shape[1] // tk
    tm, tn = min(tm, m), min(tn, n)
    return pl.pallas_call(
        functools.partial(_out_mm_kernel, nk=nk),
        grid=(n // tn, m // tm, nk),
        in_specs=[pl.BlockSpec((tm, tk), lambda j, i, k: (i, 0)),
                  pl.BlockSpec((tm, tk), lambda j, i, k: (i, 0)),
                  pl.BlockSpec((tm, tk), lambda j, i, k: (i, jnp.maximum(k - 2, 0))),
                  pl.BlockSpec((None, tk, tn), lambda j, i, k: (layer, k, j))],
        out_specs=pl.BlockSpec((tm, tn), lambda j, i, k: (i, j)),
        out_shape=jax.ShapeDtypeStruct((m, n), F32),
        scratch_shapes=[pltpu.VMEM((tm, tn), F32)],
        compiler_params=pltpu.CompilerParams(
            dimension_semantics=("parallel", "parallel", "arbitrary"),
            vmem_limit_bytes=VMEM_LIMIT),
        name="out_matmul",
    )(y_ml, y_gdn, y_ssm, w)


def _gates_kernel(x_ref, wml_ref, wgdn_ref, wssm_ref, bias_ref, alog_ref, o_ref, *maybe_xb_ref):
    pad = jnp.zeros((GATE_COLS - G_END, x_ref.shape[1]), F32)
    w_gate = jnp.concatenate([wml_ref[...], wgdn_ref[...], wssm_ref[...], pad], axis=0)
    x = x_ref[...]
    for xb_ref in maybe_xb_ref:
        xb_ref[...] = x.astype(BF16)
    raw = lax.dot_general(x, w_gate, (((1,), (1,)), ((), ())), preferred_element_type=F32,
                          precision=HIGHEST)
    col = lax.broadcasted_iota(jnp.int32, raw.shape, 1)
    u = raw + bias_ref[...]
    sp = _softplus(u)
    out = jnp.where(col < G_ML_F, u,
          jnp.where(col < G_GDN_BETA, u - sp,
          jnp.where(col < G_GDN_A, _sigmoid(u),
          jnp.where(col < G_SSM_DT, -jnp.exp(alog_ref[...]) * sp,
                    sp))))
    o_ref[...] = out


def _gates(x, w_t, layer, bias_row, alog_row, emit_bf16, tm=512):
    m, kdim = x.shape
    tm = min(tm, m)
    gate_rows = ((ML_MAIN, 2 * ML_HEADS), (ML_COLS + GDN_MAIN, 2 * GDN_HEADS),
                 (ML_COLS + GDN_COLS + SSM_MAIN, SSM_HEADS))
    assert all(s % 8 == 0 and n % 8 == 0 for s, n in gate_rows)
    wblk = lambda s, n: pl.BlockSpec((None, pl.Element(n), pl.Element(kdim)), lambda i: (layer, s, 0))
    row = pl.BlockSpec((1, GATE_COLS), lambda i: (0, 0))
    out_specs = [pl.BlockSpec((tm, GATE_COLS), lambda i: (i, 0))]
    out_shape = [jax.ShapeDtypeStruct((m, GATE_COLS), F32)]
    if emit_bf16:
        out_specs.append(pl.BlockSpec((tm, kdim), lambda i: (i, 0)))
        out_shape.append(jax.ShapeDtypeStruct((m, kdim), BF16))
    return pl.pallas_call(
        _gates_kernel,
        grid=(m // tm,),
        in_specs=[pl.BlockSpec((tm, kdim), lambda i: (i, 0))] + [wblk(s, n) for s, n in gate_rows] + [row, row],
        out_specs=out_specs,
        out_shape=out_shape,
        compiler_params=pltpu.CompilerParams(dimension_semantics=("parallel",),
                                             vmem_limit_bytes=VMEM_LIMIT),
        name="gate_proj",
    )(x, w_t, w_t, w_t, bias_row, alog_row)


ML_HB = 4


def _mlstm_kernel(q_ref, k_ref, v_ref, o_ref, z_ref, g_ref, nw_ref, out_ref,
                  c_ref, n_ref, m_ref, *, ts):
    head0 = pl.program_id(1) * ML_HB

    @pl.when(pl.program_id(2) == 0)
    def _():
        c_ref[...] = jnp.zeros_like(c_ref)
        n_ref[...] = jnp.zeros_like(n_ref)
        m_ref[...] = jnp.zeros_like(m_ref)

    causal, _, eye, upper = _chunk_masks()

    def chunk(c, carry):
        r0 = pl.multiple_of(c * CHUNK, CHUNK)
        rows = pl.ds(r0, CHUNK)
        g = g_ref[rows, :]
        heads = range(ML_HB)
        kl = [slice(hh * ML_DK, (hh + 1) * ML_DK) for hh in heads]
        vl = [slice(hh * ML_DV, (hh + 1) * ML_DV) for hh in heads]
        qf = [q_ref[rows, kl[hh]] * (ML_DK ** -0.5) for hh in heads]
        q = [qf[hh].astype(BF16) for hh in heads]
        k = [k_ref[rows, kl[hh]] for hh in heads]
        v = [v_ref[rows, vl[hh]].astype(BF16) for hh in heads]
        qk = [_dot_nt(q[hh], k[hh]) for hh in heads]
        q_c = [_dot(q[hh], c_ref[hh]) for hh in heads]
        li_c = [_gate_column(g, G_ML_I + head0 + hh) for hh in heads]
        lf_c = [_gate_column(g, G_ML_F + head0 + hh) for hh in heads]
        li_r = [jnp.sum(jnp.where(eye, li_c[hh], 0.0), axis=0, keepdims=True) for hh in heads]
        lf_r = [jnp.sum(jnp.where(eye, lf_c[hh], 0.0), axis=0, keepdims=True) for hh in heads]
        bcum_c = [jnp.sum(jnp.where(causal, lf_r[hh], 0.0), axis=1, keepdims=True) for hh in heads]
        bcum_r = [jnp.sum(jnp.where(upper, lf_c[hh], 0.0), axis=0, keepdims=True) for hh in heads]
        m_prev = [m_ref[hh] for hh in heads]
        dlog = [jnp.where(causal, bcum_c[hh] - bcum_r[hh] + li_r[hh], NEG_BIG) for hh in heads]
        inter = [bcum_c[hh] + m_prev[hh] for hh in heads]
        m_t = [jnp.maximum(inter[hh], jnp.max(dlog[hh], axis=1, keepdims=True)) for hh in heads]
        w_inter = [jnp.exp(inter[hh] - m_t[hh]) for hh in heads]
        sc = [qk[hh] * jnp.exp(dlog[hh] - m_t[hh]) for hh in heads]
        num = [_dot(sc[hh], v[hh]) + w_inter[hh] * q_c[hh] for hh in heads]
        den = [jnp.sum(sc[hh], axis=1, keepdims=True)
               + w_inter[hh] * jnp.sum(qf[hh] * n_ref[hh], axis=1, keepdims=True)
               for hh in heads]
        hval = [num[hh] / jnp.maximum(jnp.abs(den[hh]), jnp.exp(-m_t[hh])) for hh in heads]
        for hh in heads:
            b_last = bcum_c[hh][CHUNK - 1:CHUNK, :]
            gs_c = b_last - bcum_c[hh] + li_c[hh]
            gs_r = b_last - bcum_r[hh] + li_r[hh]
            m_new = jnp.maximum(b_last + m_prev[hh], jnp.max(gs_r, axis=1, keepdims=True))
            dec = jnp.exp(b_last + m_prev[hh] - m_new)
            kw = k[hh] * jnp.exp(gs_c - m_new)
            c_ref[hh] = dec * c_ref[hh] + _dot_tn(kw, v[hh])
            n_ref[hh] = dec * n_ref[hh] + jnp.sum(kw, axis=0, keepdims=True)
            m_ref[hh] = m_new
        for hh in heads:
            y = _rms_rows(hval[hh], nw_ref[:, vl[hh]])
            y = y * _sigmoid(o_ref[rows, vl[hh]]) * _silu(z_ref[rows, vl[hh]])
            out_ref[rows, vl[hh]] = y.astype(out_ref.dtype)
        return carry

    lax.fori_loop(0, ts // CHUNK, chunk, 0)


def _mlstm(proj, gact, norm_w, batch, seq):
    ts = min(SEQ_BLOCK, seq)
    ns = seq // ts
    kw_, vw = ML_HB * ML_DK, ML_HB * ML_DV
    row = lambda b, h, s: b * ns + s
    qk_blk = lambda off: pl.BlockSpec((ts, kw_), lambda b, h, s: (row(b, h, s), off // kw_ + h))
    v_blk = lambda off: pl.BlockSpec((ts, vw), lambda b, h, s: (row(b, h, s), off // vw + h))
    return pl.pallas_call(
        functools.partial(_mlstm_kernel, ts=ts),
        grid=(batch, ML_HEADS // ML_HB, ns),
        in_specs=[qk_blk(P_ML_Q), qk_blk(P_ML_K), v_blk(P_ML_V), v_blk(P_ML_O), v_blk(P_ML_Z),
                  pl.BlockSpec((ts, GATE_COLS), lambda b, h, s: (row(b, h, s), 0)),
                  pl.BlockSpec((1, vw), lambda b, h, s: (0, h))],
        out_specs=pl.BlockSpec((ts, vw), lambda b, h, s: (row(b, h, s), h)),
        out_shape=jax.ShapeDtypeStruct((batch * seq, ML_WIDTH), BF16),
        scratch_shapes=[pltpu.VMEM((ML_HB, ML_DK, ML_DV), F32), pltpu.VMEM((ML_HB, 1, ML_DK), F32),
                        pltpu.VMEM((ML_HB, 1, 1), F32)],
        compiler_params=pltpu.CompilerParams(
            dimension_semantics=("parallel", "parallel", "arbitrary"),
            vmem_limit_bytes=VMEM_LIMIT),
        name="mlstm_mixer",
    )(proj, proj, proj, proj, proj, gact, norm_w.reshape(1, ML_WIDTH))


def _conv_block(dst_ref, cur_ref, halo_ref, w_ref, first, ts, piece, post):
    back = CONV_K - 1
    for lane0 in range(0, cur_ref.shape[1], piece):
        lanes = slice(lane0, lane0 + piece)
        w = [w_ref[tap:tap + 1, lanes] for tap in range(CONV_K)]
        for r0 in range(0, ts, CHUNK):
            if r0 == 0:
                win = jnp.concatenate([jnp.where(first, 0.0, halo_ref[:, lanes]), cur_ref[0:CHUNK, lanes]],
                                      axis=0)
                taps = [win[HALO - back + tap:HALO - back + tap + CHUNK, :] for tap in range(CONV_K)]
            else:
                taps = [cur_ref[pl.ds(r0 - back + tap, CHUNK), lanes] for tap in range(CONV_K)]
            acc = w[0] * taps[0]
            for tap in range(1, CONV_K):
                acc = acc + w[tap] * taps[tap]
            dst_ref[r0:r0 + CHUNK, lanes] = post(acc, lanes)


GDN_HB = 8


def _gdn_kernel(q_ref, k_ref, v_ref, qh_ref, kh_ref, vh_ref, z_ref, g_ref,
                wq_ref, wk_ref, wv_ref, nw_ref, out_ref,
                s_ref, qs_ref, ks_ref, vs_ref, *, ts):
    head0 = pl.program_id(1) * GDN_HB
    first = pl.program_id(2) == 0
    d = GDN_HEAD_DIM

    @pl.when(first)
    def _():
        s_ref[...] = jnp.zeros_like(s_ref)

    def l2n(u):
        return u * lax.rsqrt(jnp.sum(u * u, axis=-1, keepdims=True) + RMS_EPS)

    _conv_block(ks_ref, k_ref, kh_ref, wk_ref, first, ts, d, lambda u, _: l2n(_silu(u)))
    _conv_block(qs_ref, q_ref, qh_ref, wq_ref, first, ts, d, lambda u, _: l2n(_silu(u)) * (d ** -0.5))
    _conv_block(vs_ref, v_ref, vh_ref, wv_ref, first, ts, d, lambda u, _: _silu(u))

    causal, strict, eye, upper = _chunk_masks()
    eye_f = eye.astype(F32)
    nw = nw_ref[...]

    def chunk(c, carry):
        r0 = pl.multiple_of(c * CHUNK, CHUNK)
        rows = pl.ds(r0, CHUNK)
        g = g_ref[rows, :]
        heads = range(GDN_HB)
        lanes = [slice(hh * d, (hh + 1) * d) for hh in heads]
        k = [ks_ref[rows, lanes[hh]] for hh in heads]
        q = [qs_ref[rows, lanes[hh]] for hh in heads]
        kq = [jnp.concatenate([k[hh], q[hh]], axis=0).astype(BF16) for hh in heads]
        kk_qk = [_dot_nt(kq[hh], kq[hh][:CHUNK]) for hh in heads]
        kq_s = [_dot(kq[hh], s_ref[hh]) for hh in heads]
        b_c = [_gate_column(g, G_GDN_BETA + head0 + hh) for hh in heads]
        g_c = [_gate_column(g, G_GDN_A + head0 + hh) for hh in heads]
        g_r = [jnp.sum(jnp.where(eye, g_c[hh], 0.0), axis=0, keepdims=True) for hh in heads]
        gc_c = [jnp.sum(jnp.where(causal, g_r[hh], 0.0), axis=1, keepdims=True) for hh in heads]
        gc_r = [jnp.sum(jnp.where(upper, g_c[hh], 0.0), axis=0, keepdims=True) for hh in heads]
        decay = [jnp.exp(jnp.where(causal, gc_c[hh] - gc_r[hh], NEG_BIG)) for hh in heads]
        a = [jnp.where(strict, b_c[hh] * kk_qk[hh][:CHUNK] * decay[hh], 0.0) for hh in heads]
        tinv = [eye_f - a[hh] for hh in heads]
        apow = a
        for step in range(5):
            apow = [_dot(apow[hh], apow[hh]) for hh in heads]
            tinv = [tinv[hh] + _dot(tinv[hh], apow[hh]) for hh in heads]
        v = [vs_ref[rows, lanes[hh]] for hh in heads]
        eg_c = [jnp.exp(gc_c[hh]) for hh in heads]
        v_new = [_dot(tinv[hh], b_c[hh] * (v[hh] - eg_c[hh] * kq_s[hh][:CHUNK])) for hh in heads]
        o = [eg_c[hh] * kq_s[hh][CHUNK:] + _dot(kk_qk[hh][CHUNK:] * decay[hh], v_new[hh]) for hh in heads]
        for hh in heads:
            g_last = gc_c[hh][CHUNK - 1:CHUNK, :]
            s_ref[hh] = (s_ref[hh] * jnp.exp(g_last)
                         + _dot_tn(k[hh] * jnp.exp(g_last - gc_c[hh]), v_new[hh]))
        for hh in heads:
            y = _rms_rows(o[hh], nw) * _silu(z_ref[rows, lanes[hh]])
            out_ref[rows, lanes[hh]] = y.astype(out_ref.dtype)
        return carry

    lax.fori_loop(0, ts // CHUNK, chunk, 0)


def _halo_row(b, s, ns, ts):
    return jnp.maximum((b * ns + s) * (ts // HALO) - 1, 0)


def _gdn(proj, gact, conv_w, norm_w, batch, seq):
    ts = min(SEQ_BLOCK, seq)
    ns = seq // ts
    d = GDN_HEAD_DIM
    w = GDN_HB * d
    row = lambda b, h, s: b * ns + s
    cur = lambda off: pl.BlockSpec((ts, w), lambda b, h, s: (row(b, h, s), off // w + h))
    halo = lambda off: pl.BlockSpec((HALO, w), lambda b, h, s: (_halo_row(b, s, ns, ts), off // w + h))
    cw = lambda off: pl.BlockSpec((CONV_K, w), lambda b, h, s: (0, off // w + h))
    return pl.pallas_call(
        functools.partial(_gdn_kernel, ts=ts),
        grid=(batch, GDN_HEADS // GDN_HB, ns),
        in_specs=[cur(P_GDN_Q), cur(P_GDN_K), cur(P_GDN_V),
                  halo(P_GDN_Q), halo(P_GDN_K), halo(P_GDN_V),
                  cur(P_GDN_Z),
                  pl.BlockSpec((ts, GATE_COLS), lambda b, h, s: (row(b, h, s), 0)),
                  cw(0), cw(GDN_WIDTH), cw(2 * GDN_WIDTH),
                  pl.BlockSpec((1, d), lambda b, h, s: (0, 0))],
        out_specs=pl.BlockSpec((ts, w), lambda b, h, s: (row(b, h, s), h)),
        out_shape=jax.ShapeDtypeStruct((batch * seq, GDN_WIDTH), BF16),
        scratch_shapes=[pltpu.VMEM((GDN_HB, d, d), F32)] + [pltpu.VMEM((ts, w), F32)] * 3,
        compiler_params=pltpu.CompilerParams(
            dimension_semantics=("parallel", "parallel", "arbitrary"),
            vmem_limit_bytes=VMEM_LIMIT),
        name="gdn_mixer",
    )(proj, proj, proj, proj, proj, proj, proj, gact, conv_w, conv_w, conv_w, norm_w.reshape(1, d))


SSD_GB = 2


def _ssd_kernel(z_ref, x_ref, b_ref, c_ref, xh_ref, bh_ref, ch_ref, g_ref,
                wx_ref, wb_ref, wc_ref, bx_ref, bb_ref, bc_ref, alog_ref, dskip_ref, nw_ref, out_ref,
                st_ref, bufx_ref, bufb_ref, bufc_ref, sel_ref, *, ts):
    group0 = pl.program_id(1) * SSD_GB
    first = pl.program_id(2) == 0
    gw = SSM_GROUP_WIDTH
    hd = SSM_HEAD_DIM
    n = SSM_STATE
    pairs = gw // (2 * hd)

    @pl.when(first)
    def _():
        st_ref[...] = jnp.zeros_like(st_ref)

    _conv_block(bufb_ref, b_ref, bh_ref, wb_ref, first, ts, n, lambda u, l: _silu(u + bb_ref[:, l]))
    _conv_block(bufc_ref, c_ref, ch_ref, wc_ref, first, ts, n, lambda u, l: _silu(u + bc_ref[:, l]))
    _conv_block(bufx_ref, x_ref, xh_ref, wx_ref, first, ts, 2 * n, lambda u, l: _silu(u + bx_ref[:, l]))

    sel_r = lax.broadcasted_iota(jnp.int32, (3 * GATE_COLS, gw), 0) % GATE_COLS
    sel_c = lax.broadcasted_iota(jnp.int32, (3 * GATE_COLS, gw), 1)
    for gg in range(SSD_GB):
        sel_ref[gg] = (sel_r == G_SSM_DT + (group0 + gg) * SSM_GROUP_HEADS + sel_c // hd).astype(BF16)

    ri = lax.broadcasted_iota(jnp.int32, (CHUNK, gw), 0)
    cm = lax.broadcasted_iota(jnp.int32, (CHUNK, gw), 1) % hd
    causal_w = cm <= ri
    upper_w = ri <= cm
    tri_b = _chunk_masks()[0].astype(BF16)
    lane_lo = lax.broadcasted_iota(jnp.int32, (CHUNK, 2 * hd), 1) < hd

    def chunk(c, carry):
        r0 = pl.multiple_of(c * CHUNK, CHUNK)
        rows = pl.ds(r0, CHUNK)
        g = g_ref[rows, :]
        groups = range(SSD_GB)
        xl = [slice(gg * gw, (gg + 1) * gw) for gg in groups]
        nl = [slice(gg * n, (gg + 1) * n) for gg in groups]
        g3 = _split3(g)
        cum3 = jnp.dot(tri_b, g3, preferred_element_type=F32)
        gcum = cum3[:, :GATE_COLS] + cum3[:, GATE_COLS:2 * GATE_COLS] + cum3[:, 2 * GATE_COLS:]
        both3 = jnp.concatenate([g3, _split3(gcum)], axis=0)
        expanded = [jnp.dot(both3, sel_ref[gg], preferred_element_type=F32) for gg in groups]
        neg_a = [-jnp.exp(alog_ref[:, xl[gg]]) for gg in groups]
        dt = [expanded[gg][:CHUNK] for gg in groups]
        a = [dt[gg] * neg_a[gg] for gg in groups]
        acs = [expanded[gg][CHUNK:] * neg_a[gg] for gg in groups]
        bm = [bufb_ref[rows, nl[gg]] for gg in groups]
        cmat = [bufc_ref[rows, nl[gg]] for gg in groups]
        cb2 = [_dot_nt(cmat[gg], jnp.concatenate([bm[gg], bm[gg]], axis=0)) for gg in groups]
        y_off = [_dot(cmat[gg], st_ref[gg]) for gg in groups]
        x = [bufx_ref[rows, xl[gg]] for gg in groups]
        xdt = [x[gg] * dt[gg] for gg in groups]
        acs_r = [jnp.sum(jnp.where(upper_w, a[gg], 0.0), axis=0, keepdims=True) for gg in groups]
        lm = [jnp.exp(jnp.where(causal_w, acs[gg] - acs_r[gg], NEG_BIG)) for gg in groups]
        mm = [(jnp.concatenate([cb2[gg]] * pairs, axis=1) * lm[gg]).astype(BF16) for gg in groups]
        y_diag = []
        for gg in groups:
            parts = []
            for p in range(pairs):
                xp = xdt[gg][:, p * 2 * hd:(p + 1) * 2 * hd]
                rhs = jnp.concatenate([jnp.where(lane_lo, xp, 0.0), jnp.where(lane_lo, 0.0, xp)], axis=0)
                parts.append(_dot(mm[gg][:, p * 2 * hd:(p + 1) * 2 * hd], rhs))
            y_diag.append(jnp.concatenate(parts, axis=1))
        for gg in groups:
            a_last = acs[gg][CHUNK - 1:CHUNK, :]
            st_ref[gg] = (st_ref[gg] * jnp.exp(a_last)
                          + _dot_tn(bm[gg], xdt[gg] * jnp.exp(a_last - acs[gg])))
        for gg in groups:
            y = y_diag[gg] + y_off[gg] * jnp.exp(acs[gg]) + dskip_ref[:, xl[gg]] * x[gg]
            y = y * _silu(z_ref[rows, xl[gg]])
            out_ref[rows, xl[gg]] = _rms_rows(y, nw_ref[:, xl[gg]]).astype(out_ref.dtype)
        return carry

    lax.fori_loop(0, ts // CHUNK, chunk, 0)


def _ssd(proj, gact, conv_w, conv_b, a_log, dskip, norm_w, batch, seq):
    ts = min(SEQ_BLOCK, seq)
    ns = seq // ts
    gw, n = SSD_GB * SSM_GROUP_WIDTH, SSD_GB * SSM_STATE
    row = lambda b, g, s: b * ns + s
    cur = lambda off, w: pl.BlockSpec((ts, w), lambda b, g, s: (row(b, g, s), off // w + g))
    halo = lambda off, w: pl.BlockSpec((HALO, w), lambda b, g, s: (_halo_row(b, s, ns, ts), off // w + g))
    par = lambda rows_, off, w: pl.BlockSpec((rows_, w), lambda b, g, s: (0, off // w + g))
    expand = lambda p: jnp.repeat(p, SSM_HEAD_DIM).reshape(1, SSM_WIDTH)
    conv_b = conv_b.reshape(1, -1)
    return pl.pallas_call(
        functools.partial(_ssd_kernel, ts=ts),
        grid=(batch, SSM_GROUPS // SSD_GB, ns),
        in_specs=[cur(P_SSM_Z, gw), cur(P_SSM_X, gw), cur(P_SSM_B, n), cur(P_SSM_C, n),
                  halo(P_SSM_X, gw), halo(P_SSM_B, n), halo(P_SSM_C, n),
                  pl.BlockSpec((ts, GATE_COLS), lambda b, g, s: (row(b, g, s), 0)),
                  par(CONV_K, 0, gw), par(CONV_K, P_SSM_B - P_SSM_X, n), par(CONV_K, P_SSM_C - P_SSM_X, n),
                  par(1, 0, gw), par(1, P_SSM_B - P_SSM_X, n), par(1, P_SSM_C - P_SSM_X, n),
                  par(1, 0, gw), par(1, 0, gw), par(1, 0, gw)],
        out_specs=pl.BlockSpec((ts, gw), lambda b, g, s: (row(b, g, s), g)),
        out_shape=jax.ShapeDtypeStruct((batch * seq, SSM_WIDTH), BF16),
        scratch_shapes=[pltpu.VMEM((SSD_GB, SSM_STATE, SSM_GROUP_WIDTH), F32),
                        pltpu.VMEM((ts, gw), F32), pltpu.VMEM((ts, n), F32), pltpu.VMEM((ts, n), F32),
                        pltpu.VMEM((SSD_GB, 3 * GATE_COLS, SSM_GROUP_WIDTH), BF16)],
        compiler_params=pltpu.CompilerParams(
            dimension_semantics=("parallel", "parallel", "arbitrary"),
            vmem_limit_bytes=VMEM_LIMIT),
        name="ssd_mixer",
    )(proj, proj, proj, proj, proj, proj, proj, gact, conv_w, conv_w, conv_w, conv_b, conv_b, conv_b,
      expand(a_log), expand(dskip), norm_w.reshape(1, SSM_WIDTH))


def _deepnorm_kernel(y_ref, x_ref, g_ref, b_ref, o_ref, obf_ref):
    u = DEEPNORM_ALPHA * x_ref[...] + y_ref[...]
    mu = jnp.mean(u, axis=-1, keepdims=True)
    var = jnp.mean(jnp.square(u - mu), axis=-1, keepdims=True)
    out = (u - mu) * lax.rsqrt(var + LN_EPS) * g_ref[...] + b_ref[...]
    o_ref[...] = out
    obf_ref[...] = out.astype(BF16)


def _deepnorm(y, x, ln_g, ln_b, tm=256):
    m, d = x.shape
    tm = min(tm, m)
    blk = pl.BlockSpec((tm, d), lambda i: (i, 0))
    vec = pl.BlockSpec((1, d), lambda i: (0, 0))
    return pl.pallas_call(
        _deepnorm_kernel,
        grid=(m // tm,),
        in_specs=[blk, blk, vec, vec],
        out_specs=[blk, blk],
        out_shape=[jax.ShapeDtypeStruct((m, d), F32), jax.ShapeDtypeStruct((m, d), BF16)],
        compiler_params=pltpu.CompilerParams(dimension_semantics=("parallel",),
                                             vmem_limit_bytes=VMEM_LIMIT),
        name="deepnorm_ln",
    )(y, x, ln_g.reshape(1, d), ln_b.reshape(1, d))


def _gate_rows(ml_i_bias, ml_f_bias, gdn_dt_bias, gdn_a_log, ssm_dt_bias):
    zeros = lambda n: jnp.zeros((n,), F32)
    bias = jnp.concatenate([ml_i_bias, ml_f_bias, zeros(GDN_HEADS), gdn_dt_bias, ssm_dt_bias,
                            zeros(GATE_COLS - G_END)])
    alog = jnp.concatenate([zeros(G_GDN_A), gdn_a_log, zeros(GATE_COLS - G_SSM_DT)])
    return bias.reshape(1, GATE_COLS), alog.reshape(1, GATE_COLS)


def kernel(x, w_in, w_out, ml_i_bias, ml_f_bias, ml_norm_w, gdn_conv_w, gdn_A_log, gdn_dt_bias,
           gdn_norm_w, ssm_conv_w, ssm_conv_b, ssm_A_log, ssm_dt_bias, ssm_D, ssm_norm_w, ln_g, ln_b):
    batch, seq, d = x.shape
    xf = x.reshape(batch * seq, d).astype(F32)
    xb = None
    w_t = jnp.swapaxes(w_in.astype(F32), 1, 2)
    w_o = _cast_bf16(w_out.astype(F32))
    for l in range(w_in.shape[0]):
        bias_row, alog_row = _gate_rows(ml_i_bias[l], ml_f_bias[l], gdn_dt_bias[l], gdn_A_log[l],
                                        ssm_dt_bias[l])
        if xb is None:
            gact, xb = _gates(xf, w_t, l, bias_row, alog_row, True)
        else:
            gact, = _gates(xf, w_t, l, bias_row, alog_row, False)
        y_ml = _mlstm(_proj(xb, w_t, l, 0, ML_MAIN), gact, ml_norm_w[l], batch, seq)
        y_gdn = _gdn(_proj(xb, w_t, l, ML_COLS, GDN_MAIN), gact, gdn_conv_w[l], gdn_norm_w[l],
                     batch, seq)
        y_ssm = _ssd(_proj(xb, w_t, l, ML_COLS + GDN_COLS, SSM_MAIN), gact, ssm_conv_w[l],
                     ssm_conv_b[l], ssm_A_log[l], ssm_D[l], ssm_norm_w[l], batch, seq)
        y = _out_matmul(y_ml, y_gdn, y_ssm, w_o, l)
        xf, xb = _deepnorm(y, xf, ln_g[l], ln_b[l])
    return xf.reshape(batch, seq, d).astype(x.dtype)
```

```python
import functools

import jax
import jax.numpy as jnp
from jax import lax
from jax.experimental import pallas as pl
from jax.experimental.pallas import tpu as pltpu

F32 = jnp.float32
BF16 = jnp.bfloat16

D_MODEL = 4096
DEPTH = 2
CHUNK = 64
CONV_K = 4
ML_WIDTH = D_MODEL // 2
ML_HEADS = 4
ML_DV = ML_WIDTH // ML_HEADS
ML_DK = ML_DV // 2
GDN_WIDTH = D_MODEL // 2
GDN_HEAD_DIM = 128
GDN_HEADS = GDN_WIDTH // GDN_HEAD_DIM
SSM_WIDTH = D_MODEL
SSM_HEAD_DIM = 64
SSM_HEADS = SSM_WIDTH // SSM_HEAD_DIM
SSM_GROUPS = 8
SSM_STATE = 128
SSM_GROUP_WIDTH = SSM_WIDTH // SSM_GROUPS
SSM_GROUP_HEADS = SSM_HEADS // SSM_GROUPS
MIX_WIDTH = ML_WIDTH + GDN_WIDTH + SSM_WIDTH
DEEPNORM_ALPHA = (2 * DEPTH) ** 0.25
RMS_EPS = 1e-6
LN_EPS = 1e-5
NEG_BIG = -1e30

ML_MAIN = ML_HEADS * ML_DK * 2 + 3 * ML_WIDTH
ML_COLS = ML_MAIN + 2 * ML_HEADS
GDN_MAIN = 4 * GDN_WIDTH
GDN_COLS = GDN_MAIN + 2 * GDN_HEADS
SSM_MAIN = 2 * SSM_WIDTH + 2 * SSM_GROUPS * SSM_STATE
SSM_COLS = SSM_MAIN + SSM_HEADS

P_ML_Q = 0
P_ML_K = P_ML_Q + ML_HEADS * ML_DK
P_ML_V = P_ML_K + ML_HEADS * ML_DK
P_ML_O = P_ML_V + ML_WIDTH
P_ML_Z = P_ML_O + ML_WIDTH
P_GDN_Q = 0
P_GDN_K = P_GDN_Q + GDN_WIDTH
P_GDN_V = P_GDN_K + GDN_WIDTH
P_GDN_Z = P_GDN_V + GDN_WIDTH
P_SSM_Z = 0
P_SSM_X = P_SSM_Z + SSM_WIDTH
P_SSM_B = P_SSM_X + SSM_WIDTH
P_SSM_C = P_SSM_B + SSM_GROUPS * SSM_STATE

GATE_COLS = 128
G_ML_I = 0
G_ML_F = G_ML_I + ML_HEADS
G_GDN_BETA = G_ML_F + ML_HEADS
G_GDN_A = G_GDN_BETA + GDN_HEADS
G_SSM_DT = G_GDN_A + GDN_HEADS
G_END = G_SSM_DT + SSM_HEADS

HALO = 8
SEQ_BLOCK = 512
VMEM_LIMIT = 56 * 1024 * 1024


def _sigmoid(u):
    return 0.5 * jnp.tanh(0.5 * u) + 0.5


def _silu(u):
    h = 0.5 * u
    return h * jnp.tanh(h) + h


def _softplus(u):
    return jnp.maximum(u, 0.0) + jnp.log1p(jnp.exp(-jnp.abs(u)))


def _dot(a, b):
    return jnp.dot(a.astype(BF16), b.astype(BF16), preferred_element_type=F32)


def _dot_nt(a, b):
    return lax.dot_general(a.astype(BF16), b.astype(BF16), (((1,), (1,)), ((), ())),
                           preferred_element_type=F32)


def _dot_tn(a, b):
    return lax.dot_general(a.astype(BF16), b.astype(BF16), (((0,), (0,)), ((), ())),
                           preferred_element_type=F32)


def _split3(u):
    hi = u.astype(BF16)
    rest = u - hi.astype(F32)
    mid = rest.astype(BF16)
    lo = (rest - mid.astype(F32)).astype(BF16)
    return jnp.concatenate([hi, mid, lo], axis=1)


def _rms_rows(u, w_row):
    return u * lax.rsqrt(jnp.mean(u * u, axis=-1, keepdims=True) + RMS_EPS) * w_row


def _chunk_masks():
    ri = lax.broadcasted_iota(jnp.int32, (CHUNK, CHUNK), 0)
    ci = lax.broadcasted_iota(jnp.int32, (CHUNK, CHUNK), 1)
    return ci <= ri, ci < ri, ci == ri, ri <= ci


def _gate_column(g, col):
    lane = lax.broadcasted_iota(jnp.int32, g.shape, 1)
    return jnp.sum(jnp.where(lane == col, g, 0.0), axis=1, keepdims=True)


LANES = 128
PACK_ROWS = 1024
PACK_COLS = 512
PACK_SUB = 256


def _cast_kernel(w_ref, o_ref):
    def body(r, carry):
        rows = pl.ds(pl.multiple_of(r * PACK_SUB, PACK_SUB), PACK_SUB)
        o_ref[rows, :] = w_ref[rows, :].astype(BF16)
        return carry

    lax.fori_loop(0, PACK_ROWS // PACK_SUB, body, 0)


def _cast_bf16(w):
    layers, rows, cols = w.shape
    assert rows % PACK_ROWS == 0 and cols % PACK_COLS == 0
    blk = pl.BlockSpec((None, PACK_ROWS, PACK_COLS), lambda l, r, j: (l, r, j))
    return pl.pallas_call(
        _cast_kernel,
        grid=(layers, rows // PACK_ROWS, cols // PACK_COLS),
        in_specs=[blk],
        out_specs=blk,
        out_shape=jax.ShapeDtypeStruct((layers, rows, cols), BF16),
        compiler_params=pltpu.CompilerParams(
            dimension_semantics=("parallel", "parallel", "parallel"), vmem_limit_bytes=VMEM_LIMIT),
        name="cast_weights",
    )(w)


PROJ_TM = 512
PROJ_TN = 1024


def _proj_kernel(x_ref, wt_ref, o_ref, wbf_ref):
    @pl.when(pl.program_id(1) == 0)
    def _():
        def body(r, carry):
            rows = pl.ds(pl.multiple_of(r * LANES, LANES), LANES)
            wbf_ref[rows, :] = wt_ref[rows, :].astype(BF16)
            return carry

        lax.fori_loop(0, wt_ref.shape[0] // LANES, body, 0)

    o_ref[...] = lax.dot_general(x_ref[...], wbf_ref[...], (((1,), (1,)), ((), ())),
                                 preferred_element_type=F32)


def _proj(x, w_t, layer, start, width):
    m, kdim = x.shape
    tm, tn = min(PROJ_TM, m), PROJ_TN
    assert start % 8 == 0 and width % tn == 0 and m % tm == 0
    return pl.pallas_call(
        _proj_kernel,
        grid=(width // tn, m // tm),
        in_specs=[pl.BlockSpec((tm, kdim), lambda j, i: (i, 0)),
                  pl.BlockSpec((None, pl.Element(tn), pl.Element(kdim)),
                               lambda j, i: (layer, (start // 8 + j * (tn // 8)) * 8, 0))],
        out_specs=pl.BlockSpec((tm, tn), lambda j, i: (i, j)),
        out_shape=jax.ShapeDtypeStruct((m, width), F32),
        scratch_shapes=[pltpu.VMEM((tn, kdim), BF16)],
        compiler_params=pltpu.CompilerParams(
            dimension_semantics=("parallel", "arbitrary"), vmem_limit_bytes=VMEM_LIMIT),
        name="proj_matmul",
    )(x, w_t)


def _out_mm_kernel(yml_ref, ygdn_ref, yssm_ref, w_ref, o_ref, acc_ref, *, nk):
    k = pl.program_id(2)

    @pl.when(k == 0)
    def _():
        acc_ref[...] = jnp.dot(yml_ref[...], w_ref[...], preferred_element_type=F32)

    @pl.when(k == 1)
    def _():
        acc_ref[...] += jnp.dot(ygdn_ref[...], w_ref[...], preferred_element_type=F32)

    @pl.when(jnp.logical_and(k >= 2, k < nk - 1))
    def _():
        acc_ref[...] += jnp.dot(yssm_ref[...], w_ref[...], preferred_element_type=F32)

    @pl.when(k == nk - 1)
    def _():
        o_ref[...] = acc_ref[...] + jnp.dot(yssm_ref[...], w_ref[...], preferred_element_type=F32)


def _out_matmul(y_ml, y_gdn, y_ssm, w, layer, tm=1024, tn=1024):
    m = y_ml.shape[0]
    n = w.shape[2]
    tk = ML_WIDTH
    assert y_ml.shape[1] == tk and y_gdn.shape[1] == tk and y_ssm.shape[1] % tk == 0
    nk = w.shape[1] // tk
    tm, tn = min(tm, m), min(tn, n)
    return pl.pallas_call(
        functools.partial(_out_mm_kernel, nk=nk),
        grid=(n // tn, m // tm, nk),
        in_specs=[pl.BlockSpec((tm, tk), lambda j, i, k: (i, 0)),
                  pl.BlockSpec((tm, tk), lambda j, i, k: (i, 0)),
                  pl.BlockSpec((tm, tk), lambda j, i, k: (i, jnp.maximum(k - 2, 0))),
                  pl.BlockSpec((None, tk, tn), lambda j, i, k: (layer, k, j))],
        out_specs=pl.BlockSpec((tm, tn), lambda j, i, k: (i, j)),
        out_shape=jax.ShapeDtypeStruct((m, n), F32),
        scratch_shapes=[pltpu.VMEM((tm, tn), F32)],
        compiler_params=pltpu.CompilerParams(
            dimension_semantics=("parallel", "parallel", "arbitrary"),
            vmem_limit_bytes=VMEM_LIMIT),
        name="out_matmul",
    )(y_ml, y_gdn, y_ssm, w)


def _gates_kernel(x_ref, wml_ref, wgdn_ref, wssm_ref, bias_ref, alog_ref, o_ref, *maybe_xb_ref):
    pad = jnp.zeros((GATE_COLS - G_END, x_ref.shape[1]), F32)
    w_gate = jnp.concatenate([wml_ref[...], wgdn_ref[...], wssm_ref[...], pad], axis=0)
    x = x_ref[...]
    x_hi = x.astype(BF16)
    x_lo = (x - x_hi.astype(F32)).astype(BF16)
    w_hi = w_gate.astype(BF16)
    w_lo = (w_gate - w_hi.astype(F32)).astype(BF16)
    for xb_ref in maybe_xb_ref:
        xb_ref[...] = x_hi
    raw = _dot_nt(x_hi, w_hi) + (_dot_nt(x_hi, w_lo) + _dot_nt(x_lo, w_hi))
    col = lax.broadcasted_iota(jnp.int32, raw.shape, 1)
    u = raw + bias_ref[...]
    sp = _softplus(u)
    out = jnp.where(col < G_ML_F, u,
          jnp.where(col < G_GDN_BETA, u - sp,
          jnp.where(col < G_GDN_A, _sigmoid(u),
          jnp.where(col < G_SSM_DT, -jnp.exp(alog_ref[...]) * sp,
                    sp))))
    o_ref[...] = out


def _gates(x, w_t, layer, bias_row, alog_row, emit_bf16, tm=512):
    m, kdim = x.shape
    tm = min(tm, m)
    gate_rows = ((ML_MAIN, 2 * ML_HEADS), (ML_COLS + GDN_MAIN, 2 * GDN_HEADS),
                 (ML_COLS + GDN_COLS + SSM_MAIN, SSM_HEADS))
    assert all(s % 8 == 0 and n % 8 == 0 for s, n in gate_rows)
    wblk = lambda s, n: pl.BlockSpec((None, pl.Element(n), pl.Element(kdim)), lambda i: (layer, s, 0))
    row = pl.BlockSpec((1, GATE_COLS), lambda i: (0, 0))
    out_specs = [pl.BlockSpec((tm, GATE_COLS), lambda i: (i, 0))]
    out_shape = [jax.ShapeDtypeStruct((m, GATE_COLS), F32)]
    if emit_bf16:
        out_specs.append(pl.BlockSpec((tm, kdim), lambda i: (i, 0)))
        out_shape.append(jax.ShapeDtypeStruct((m, kdim), BF16))
    return pl.pallas_call(
        _gates_kernel,
        grid=(m // tm,),
        in_specs=[pl.BlockSpec((tm, kdim), lambda i: (i, 0))] + [wblk(s, n) for s, n in gate_rows] + [row, row],
        out_specs=out_specs,
        out_shape=out_shape,
        compiler_params=pltpu.CompilerParams(dimension_semantics=("parallel",),
                                             vmem_limit_bytes=VMEM_LIMIT),
        name="gate_proj",
    )(x, w_t, w_t, w_t, bias_row, alog_row)


ML_HB = 4


def _mlstm_kernel(q_ref, k_ref, v_ref, o_ref, z_ref, g_ref, nw_ref, out_ref,
                  c_ref, n_ref, m_ref, *, ts):
    head0 = pl.program_id(1) * ML_HB

    @pl.when(pl.program_id(2) == 0)
    def _():
        c_ref[...] = jnp.zeros_like(c_ref)
        n_ref[...] = jnp.zeros_like(n_ref)
        m_ref[...] = jnp.zeros_like(m_ref)

    causal, _, eye, upper = _chunk_masks()

    def chunk(c, carry):
        r0 = pl.multiple_of(c * CHUNK, CHUNK)
        rows = pl.ds(r0, CHUNK)
        g = g_ref[rows, :]
        heads = range(ML_HB)
        kl = [slice(hh * ML_DK, (hh + 1) * ML_DK) for hh in heads]
        vl = [slice(hh * ML_DV, (hh + 1) * ML_DV) for hh in heads]
        qf = [q_ref[rows, kl[hh]] * (ML_DK ** -0.5) for hh in heads]
        q = [qf[hh].astype(BF16) for hh in heads]
        k = [k_ref[rows, kl[hh]] for hh in heads]
        v = [v_ref[rows, vl[hh]].astype(BF16) for hh in heads]
        qk = [_dot_nt(q[hh], k[hh]) for hh in heads]
        q_c = [_dot(q[hh], c_ref[hh]) for hh in heads]
        li_c = [_gate_column(g, G_ML_I + head0 + hh) for hh in heads]
        lf_c = [_gate_column(g, G_ML_F + head0 + hh) for hh in heads]
        li_r = [jnp.sum(jnp.where(eye, li_c[hh], 0.0), axis=0, keepdims=True) for hh in heads]
        lf_r = [jnp.sum(jnp.where(eye, lf_c[hh], 0.0), axis=0, keepdims=True) for hh in heads]
        bcum_c = [jnp.sum(jnp.where(causal, lf_r[hh], 0.0), axis=1, keepdims=True) for hh in heads]
        bcum_r = [jnp.sum(jnp.where(upper, lf_c[hh], 0.0), axis=0, keepdims=True) for hh in heads]
        m_prev = [m_ref[hh] for hh in heads]
        dlog = [jnp.where(causal, bcum_c[hh] - bcum_r[hh] + li_r[hh], NEG_BIG) for hh in heads]
        inter = [bcum_c[hh] + m_prev[hh] for hh in heads]
        m_t = [jnp.maximum(inter[hh], jnp.max(dlog[hh], axis=1, keepdims=True)) for hh in heads]
        w_inter = [jnp.exp(inter[hh] - m_t[hh]) for hh in heads]
        sc = [qk[hh] * jnp.exp(dlog[hh] - m_t[hh]) for hh in heads]
        num = [_dot(sc[hh], v[hh]) + w_inter[hh] * q_c[hh] for hh in heads]
        den = [jnp.sum(sc[hh], axis=1, keepdims=True)
               + w_inter[hh] * jnp.sum(qf[hh] * n_ref[hh], axis=1, keepdims=True)
               for hh in heads]
        inv = [1.0 / jnp.maximum(jnp.abs(den[hh]), jnp.exp(-m_t[hh])) for hh in heads]
        hval = [num[hh] * inv[hh] for hh in heads]
        for hh in heads:
            b_last = bcum_c[hh][CHUNK - 1:CHUNK, :]
            gs_c = b_last - bcum_c[hh] + li_c[hh]
            gs_r = b_last - bcum_r[hh] + li_r[hh]
            m_new = jnp.maximum(b_last + m_prev[hh], jnp.max(gs_r, axis=1, keepdims=True))
            dec = jnp.exp(b_last + m_prev[hh] - m_new)
            kw = k[hh] * jnp.exp(gs_c - m_new)
            c_ref[hh] = dec * c_ref[hh] + _dot_tn(kw, v[hh])
            n_ref[hh] = dec * n_ref[hh] + jnp.sum(kw, axis=0, keepdims=True)
            m_ref[hh] = m_new
        for hh in heads:
            y = _rms_rows(hval[hh], nw_ref[:, vl[hh]])
            y = y * _sigmoid(o_ref[rows, vl[hh]]) * _silu(z_ref[rows, vl[hh]])
            out_ref[rows, vl[hh]] = y.astype(out_ref.dtype)
        return carry

    lax.fori_loop(0, ts // CHUNK, chunk, 0)


def _mlstm(proj, gact, norm_w, batch, seq):
    ts = min(SEQ_BLOCK, seq)
    ns = seq // ts
    kw_, vw = ML_HB * ML_DK, ML_HB * ML_DV
    row = lambda b, h, s: b * ns + s
    qk_blk = lambda off: pl.BlockSpec((ts, kw_), lambda b, h, s: (row(b, h, s), off // kw_ + h))
    v_blk = lambda off: pl.BlockSpec((ts, vw), lambda b, h, s: (row(b, h, s), off // vw + h))
    return pl.pallas_call(
        functools.partial(_mlstm_kernel, ts=ts),
        grid=(batch, ML_HEADS // ML_HB, ns),
        in_specs=[qk_blk(P_ML_Q), qk_blk(P_ML_K), v_blk(P_ML_V), v_blk(P_ML_O), v_blk(P_ML_Z),
                  pl.BlockSpec((ts, GATE_COLS), lambda b, h, s: (row(b, h, s), 0)),
                  pl.BlockSpec((1, vw), lambda b, h, s: (0, h))],
        out_specs=pl.BlockSpec((ts, vw), lambda b, h, s: (row(b, h, s), h)),
        out_shape=jax.ShapeDtypeStruct((batch * seq, ML_WIDTH), BF16),
        scratch_shapes=[pltpu.VMEM((ML_HB, ML_DK, ML_DV), F32), pltpu.VMEM((ML_HB, 1, ML_DK), F32),
                        pltpu.VMEM((ML_HB, 1, 1), F32)],
        compiler_params=pltpu.CompilerParams(
            dimension_semantics=("parallel", "parallel", "arbitrary"),
            vmem_limit_bytes=VMEM_LIMIT),
        name="mlstm_mixer",
    )(proj, proj, proj, proj, proj, gact, norm_w.reshape(1, ML_WIDTH))


def _conv_block(dst_ref, cur_ref, halo_ref, w_ref, first, ts, piece, post):
    back = CONV_K - 1
    for lane0 in range(0, cur_ref.shape[1], piece):
        lanes = slice(lane0, lane0 + piece)
        w = [w_ref[tap:tap + 1, lanes] for tap in range(CONV_K)]
        for r0 in range(0, ts, CHUNK):
            if r0 == 0:
                win = jnp.concatenate([jnp.where(first, 0.0, halo_ref[:, lanes]), cur_ref[0:CHUNK, lanes]],
                                      axis=0)
                taps = [win[HALO - back + tap:HALO - back + tap + CHUNK, :] for tap in range(CONV_K)]
            else:
                taps = [cur_ref[pl.ds(r0 - back + tap, CHUNK), lanes] for tap in range(CONV_K)]
            acc = w[0] * taps[0]
            for tap in range(1, CONV_K):
                acc = acc + w[tap] * taps[tap]
            dst_ref[r0:r0 + CHUNK, lanes] = post(acc, lanes)


GDN_HB = 16


def _gdn_kernel(q_ref, k_ref, v_ref, qh_ref, kh_ref, vh_ref, z_ref, g_ref,
                wq_ref, wk_ref, wv_ref, nw_ref, out_ref,
                s_ref, qs_ref, ks_ref, vs_ref, *, ts):
    head0 = pl.program_id(1) * GDN_HB
    first = pl.program_id(2) == 0
    d = GDN_HEAD_DIM

    @pl.when(first)
    def _():
        s_ref[...] = jnp.zeros_like(s_ref)

    def l2n(u):
        return u * lax.rsqrt(jnp.sum(u * u, axis=-1, keepdims=True) + RMS_EPS)

    _conv_block(ks_ref, k_ref, kh_ref, wk_ref, first, ts, d, lambda u, _: l2n(_silu(u)))
    _conv_block(qs_ref, q_ref, qh_ref, wq_ref, first, ts, d, lambda u, _: l2n(_silu(u)) * (d ** -0.5))
    _conv_block(vs_ref, v_ref, vh_ref, wv_ref, first, ts, d, lambda u, _: _silu(u))

    causal, strict, eye, upper = _chunk_masks()
    eye_f = eye.astype(F32)
    nw = nw_ref[...]

    def chunk(c, carry):
        r0 = pl.multiple_of(c * CHUNK, CHUNK)
        rows = pl.ds(r0, CHUNK)
        g = g_ref[rows, :]
        heads = range(GDN_HB)
        lanes = [slice(hh * d, (hh + 1) * d) for hh in heads]
        k = [ks_ref[rows, lanes[hh]] for hh in heads]
        q = [qs_ref[rows, lanes[hh]] for hh in heads]
        kq = [jnp.concatenate([k[hh], q[hh]], axis=0).astype(BF16) for hh in heads]
        kk_qk = [_dot_nt(kq[hh], kq[hh][:CHUNK]) for hh in heads]
        kq_s = [_dot(kq[hh], s_ref[hh]) for hh in heads]
        b_c = [_gate_column(g, G_GDN_BETA + head0 + hh) for hh in heads]
        g_c = [_gate_column(g, G_GDN_A + head0 + hh) for hh in heads]
        g_r = [jnp.sum(jnp.where(eye, g_c[hh], 0.0), axis=0, keepdims=True) for hh in heads]
        gc_c = [jnp.sum(jnp.where(causal, g_r[hh], 0.0), axis=1, keepdims=True) for hh in heads]
        gc_r = [jnp.sum(jnp.where(upper, g_c[hh], 0.0), axis=0, keepdims=True) for hh in heads]
        decay = [jnp.exp(jnp.where(causal, gc_c[hh] - gc_r[hh], NEG_BIG)) for hh in heads]
        a = [jnp.where(strict, b_c[hh] * kk_qk[hh][:CHUNK] * decay[hh], 0.0) for hh in heads]
        tinv = [eye_f - a[hh] for hh in heads]
        apow = a
        for step in range(5):
            apow = [_dot(apow[hh], apow[hh]) for hh in heads]
            tinv = [tinv[hh] + _dot(tinv[hh], apow[hh]) for hh in heads]
        v = [vs_ref[rows, lanes[hh]] for hh in heads]
        eg_c = [jnp.exp(gc_c[hh]) for hh in heads]
        v_new = [_dot(tinv[hh], b_c[hh] * (v[hh] - eg_c[hh] * kq_s[hh][:CHUNK])) for hh in heads]
        o = [eg_c[hh] * kq_s[hh][CHUNK:] + _dot(kk_qk[hh][CHUNK:] * decay[hh], v_new[hh]) for hh in heads]
        for hh in heads:
            g_last = gc_c[hh][CHUNK - 1:CHUNK, :]
            s_ref[hh] = (s_ref[hh] * jnp.exp(g_last)
                         + _dot_tn(k[hh] * jnp.exp(g_last - gc_c[hh]), v_new[hh]))
        for hh in heads:
            y = _rms_rows(o[hh], nw) * _silu(z_ref[rows, lanes[hh]])
            out_ref[rows, lanes[hh]] = y.astype(out_ref.dtype)
        return carry

    lax.fori_loop(0, ts // CHUNK, chunk, 0)


def _halo_row(b, s, ns, ts):
    return jnp.maximum((b * ns + s) * (ts // HALO) - 1, 0)


def _gdn(proj, gact, conv_w, norm_w, batch, seq):
    ts = min(SEQ_BLOCK, seq)
    ns = seq // ts
    d = GDN_HEAD_DIM
    w = GDN_HB * d
    row = lambda b, h, s: b * ns + s
    cur = lambda off: pl.BlockSpec((ts, w), lambda b, h, s: (row(b, h, s), off // w + h))
    halo = lambda off: pl.BlockSpec((HALO, w), lambda b, h, s: (_halo_row(b, s, ns, ts), off // w + h))
    cw = lambda off: pl.BlockSpec((CONV_K, w), lambda b, h, s: (0, off // w + h))
    return pl.pallas_call(
        functools.partial(_gdn_kernel, ts=ts),
        grid=(batch, GDN_HEADS // GDN_HB, ns),
        in_specs=[cur(P_GDN_Q), cur(P_GDN_K), cur(P_GDN_V),
                  halo(P_GDN_Q), halo(P_GDN_K), halo(P_GDN_V),
                  cur(P_GDN_Z),
                  pl.BlockSpec((ts, GATE_COLS), lambda b, h, s: (row(b, h, s), 0)),
                  cw(0), cw(GDN_WIDTH), cw(2 * GDN_WIDTH),
                  pl.BlockSpec((1, d), lambda b, h, s: (0, 0))],
        out_specs=pl.BlockSpec((ts, w), lambda b, h, s: (row(b, h, s), h)),
        out_shape=jax.ShapeDtypeStruct((batch * seq, GDN_WIDTH), BF16),
        scratch_shapes=[pltpu.VMEM((GDN_HB, d, d), F32)] + [pltpu.VMEM((ts, w), F32)] * 3,
        compiler_params=pltpu.CompilerParams(
            dimension_semantics=("parallel", "parallel", "arbitrary"),
            vmem_limit_bytes=VMEM_LIMIT),
        name="gdn_mixer",
    )(proj, proj, proj, proj, proj, proj, proj, gact, conv_w, conv_w, conv_w, norm_w.reshape(1, d))


SSD_GB = 4


def _ssd_kernel(z_ref, x_ref, b_ref, c_ref, xh_ref, bh_ref, ch_ref, g_ref,
                wx_ref, wb_ref, wc_ref, bx_ref, bb_ref, bc_ref, alog_ref, dskip_ref, nw_ref, out_ref,
                st_ref, bufx_ref, bufb_ref, bufc_ref, sel_ref, *, ts):
    group0 = pl.program_id(1) * SSD_GB
    first = pl.program_id(2) == 0
    gw = SSM_GROUP_WIDTH
    hd = SSM_HEAD_DIM
    n = SSM_STATE
    pairs = gw // (2 * hd)

    @pl.when(first)
    def _():
        st_ref[...] = jnp.zeros_like(st_ref)

    _conv_block(bufb_ref, b_ref, bh_ref, wb_ref, first, ts, n, lambda u, l: _silu(u + bb_ref[:, l]))
    _conv_block(bufc_ref, c_ref, ch_ref, wc_ref, first, ts, n, lambda u, l: _silu(u + bc_ref[:, l]))
    _conv_block(bufx_ref, x_ref, xh_ref, wx_ref, first, ts, 2 * n, lambda u, l: _silu(u + bx_ref[:, l]))

    sel_r = lax.broadcasted_iota(jnp.int32, (3 * GATE_COLS, gw), 0) % GATE_COLS
    sel_c = lax.broadcasted_iota(jnp.int32, (3 * GATE_COLS, gw), 1)
    for gg in range(SSD_GB):
        sel_ref[gg] = (sel_r == G_SSM_DT + (group0 + gg) * SSM_GROUP_HEADS + sel_c // hd).astype(BF16)

    ri = lax.broadcasted_iota(jnp.int32, (CHUNK, gw), 0)
    cm = lax.broadcasted_iota(jnp.int32, (CHUNK, gw), 1) % hd
    causal_w = cm <= ri
    upper_w = ri <= cm
    tri_b = _chunk_masks()[0].astype(BF16)
    lane_lo = lax.broadcasted_iota(jnp.int32, (CHUNK, 2 * hd), 1) < hd

    def chunk(c, carry):
        r0 = pl.multiple_of(c * CHUNK, CHUNK)
        rows = pl.ds(r0, CHUNK)
        g = g_ref[rows, :]
        groups = range(SSD_GB)
        xl = [slice(gg * gw, (gg + 1) * gw) for gg in groups]
        nl = [slice(gg * n, (gg + 1) * n) for gg in groups]
        g3 = _split3(g)
        cum3 = jnp.dot(tri_b, g3, preferred_element_type=F32)
        gcum = cum3[:, :GATE_COLS] + cum3[:, GATE_COLS:2 * GATE_COLS] + cum3[:, 2 * GATE_COLS:]
        both3 = jnp.concatenate([g3, _split3(gcum)], axis=0)
        expanded = [jnp.dot(both3, sel_ref[gg], preferred_element_type=F32) for gg in groups]
        neg_a = [-jnp.exp(alog_ref[:, xl[gg]]) for gg in groups]
        dt = [expanded[gg][:CHUNK] for gg in groups]
        a = [dt[gg] * neg_a[gg] for gg in groups]
        acs = [expanded[gg][CHUNK:] * neg_a[gg] for gg in groups]
        bm = [bufb_ref[rows, nl[gg]] for gg in groups]
        cmat = [bufc_ref[rows, nl[gg]] for gg in groups]
        cb2 = [_dot_nt(cmat[gg], jnp.concatenate([bm[gg], bm[gg]], axis=0)) for gg in groups]
        y_off = [_dot(cmat[gg], st_ref[gg]) for gg in groups]
        x = [bufx_ref[rows, xl[gg]] for gg in groups]
        xdt = [x[gg] * dt[gg] for gg in groups]
        acs_r = [jnp.sum(jnp.where(upper_w, a[gg], 0.0), axis=0, keepdims=True) for gg in groups]
        lm = [jnp.exp(jnp.where(causal_w, acs[gg] - acs_r[gg], NEG_BIG)) for gg in groups]
        mm = [(jnp.concatenate([cb2[gg]] * pairs, axis=1) * lm[gg]).astype(BF16) for gg in groups]
        y_diag = []
        for gg in groups:
            parts = []
            for p in range(pairs):
                xp = xdt[gg][:, p * 2 * hd:(p + 1) * 2 * hd]
                rhs = jnp.concatenate([jnp.where(lane_lo, xp, 0.0), jnp.where(lane_lo, 0.0, xp)], axis=0)
                parts.append(_dot(mm[gg][:, p * 2 * hd:(p + 1) * 2 * hd], rhs))
            y_diag.append(jnp.concatenate(parts, axis=1))
        for gg in groups:
            a_last = acs[gg][CHUNK - 1:CHUNK, :]
            st_ref[gg] = (st_ref[gg] * jnp.exp(a_last)
                          + _dot_tn(bm[gg], xdt[gg] * jnp.exp(a_last - acs[gg])))
        for gg in groups:
            y = y_diag[gg] + y_off[gg] * jnp.exp(acs[gg]) + dskip_ref[:, xl[gg]] * x[gg]
            y = y * _silu(z_ref[rows, xl[gg]])
            out_ref[rows, xl[gg]] = _rms_rows(y, nw_ref[:, xl[gg]]).astype(out_ref.dtype)
        return carry

    lax.fori_loop(0, ts // CHUNK, chunk, 0)


def _ssd(proj, gact, conv_w, conv_b, a_log, dskip, norm_w, batch, seq):
    ts = min(SEQ_BLOCK, seq)
    ns = seq // ts
    gw, n = SSD_GB * SSM_GROUP_WIDTH, SSD_GB * SSM_STATE
    row = lambda b, g, s: b * ns + s
    cur = lambda off, w: pl.BlockSpec((ts, w), lambda b, g, s: (row(b, g, s), off // w + g))
    halo = lambda off, w: pl.BlockSpec((HALO, w), lambda b, g, s: (_halo_row(b, s, ns, ts), off // w + g))
    par = lambda rows_, off, w: pl.BlockSpec((rows_, w), lambda b, g, s: (0, off // w + g))
    expand = lambda p: jnp.repeat(p, SSM_HEAD_DIM).reshape(1, SSM_WIDTH)
    conv_b = conv_b.reshape(1, -1)
    return pl.pallas_call(
        functools.partial(_ssd_kernel, ts=ts),
        grid=(batch, SSM_GROUPS // SSD_GB, ns),
        in_specs=[cur(P_SSM_Z, gw), cur(P_SSM_X, gw), cur(P_SSM_B, n), cur(P_SSM_C, n),
                  halo(P_SSM_X, gw), halo(P_SSM_B, n), halo(P_SSM_C, n),
                  pl.BlockSpec((ts, GATE_COLS), lambda b, g, s: (row(b, g, s), 0)),
                  par(CONV_K, 0, gw), par(CONV_K, P_SSM_B - P_SSM_X, n), par(CONV_K, P_SSM_C - P_SSM_X, n),
                  par(1, 0, gw), par(1, P_SSM_B - P_SSM_X, n), par(1, P_SSM_C - P_SSM_X, n),
                  par(1, 0, gw), par(1, 0, gw), par(1, 0, gw)],
        out_specs=pl.BlockSpec((ts, gw), lambda b, g, s: (row(b, g, s), g)),
        out_shape=jax.ShapeDtypeStruct((batch * seq, SSM_WIDTH), BF16),
        scratch_shapes=[pltpu.VMEM((SSD_GB, SSM_STATE, SSM_GROUP_WIDTH), F32),
                        pltpu.VMEM((ts, gw), F32), pltpu.VMEM((ts, n), F32), pltpu.VMEM((ts, n), F32),
                        pltpu.VMEM((SSD_GB, 3 * GATE_COLS, SSM_GROUP_WIDTH), BF16)],
        compiler_params=pltpu.CompilerParams(
            dimension_semantics=("parallel", "parallel", "arbitrary"),
            vmem_limit_bytes=VMEM_LIMIT),
        name="ssd_mixer",
    )(proj, proj, proj, proj, proj, proj, proj, gact, conv_w, conv_w, conv_w, conv_b, conv_b, conv_b,
      expand(a_log), expand(dskip), norm_w.reshape(1, SSM_WIDTH))


def _deepnorm_kernel(y_ref, x_ref, g_ref, b_ref, o_ref, obf_ref):
    u = DEEPNORM_ALPHA * x_ref[...] + y_ref[...]
    mu = jnp.mean(u, axis=-1, keepdims=True)
    var = jnp.mean(jnp.square(u - mu), axis=-1, keepdims=True)
    out = (u - mu) * lax.rsqrt(var + LN_EPS) * g_ref[...] + b_ref[...]
    o_ref[...] = out
    obf_ref[...] = out.astype(BF16)


def _deepnorm(y, x, ln_g, ln_b, tm=256):
    m, d = x.shape
    tm = min(tm, m)
    blk = pl.BlockSpec((tm, d), lambda i: (i, 0))
    vec = pl.BlockSpec((1, d), lambda i: (0, 0))
    return pl.pallas_call(
        _deepnorm_kernel,
        grid=(m // tm,),
        in_specs=[blk, blk, vec, vec],
        out_specs=[blk, blk],
        out_shape=[jax.ShapeDtypeStruct((m, d), F32), jax.ShapeDtypeStruct((m, d), BF16)],
        compiler_params=pltpu.CompilerParams(dimension_semantics=("parallel",),
                                             vmem_limit_bytes=VMEM_LIMIT),
        name="deepnorm_ln",
    )(y, x, ln_g.reshape(1, d), ln_b.reshape(1, d))


def _gate_rows(ml_i_bias, ml_f_bias, gdn_dt_bias, gdn_a_log, ssm_dt_bias):
    zeros = lambda n: jnp.zeros((n,), F32)
    bias = jnp.concatenate([ml_i_bias, ml_f_bias, zeros(GDN_HEADS), gdn_dt_bias, ssm_dt_bias,
                            zeros(GATE_COLS - G_END)])
    alog = jnp.concatenate([zeros(G_GDN_A), gdn_a_log, zeros(GATE_COLS - G_SSM_DT)])
    return bias.reshape(1, GATE_COLS), alog.reshape(1, GATE_COLS)


def kernel(x, w_in, w_out, ml_i_bias, ml_f_bias, ml_norm_w, gdn_conv_w, gdn_A_log, gdn_dt_bias,
           gdn_norm_w, ssm_conv_w, ssm_conv_b, ssm_A_log, ssm_dt_bias, ssm_D, ssm_norm_w, ln_g, ln_b):
    batch, seq, d = x.shape
    xf = x.reshape(batch * seq, d).astype(F32)
    xb = None
    w_t = jnp.swapaxes(w_in.astype(F32), 1, 2)
    w_o = _cast_bf16(w_out.astype(F32))
    for l in range(w_in.shape[0]):
        bias_row, alog_row = _gate_rows(ml_i_bias[l], ml_f_bias[l], gdn_dt_bias[l], gdn_A_log[l],
                                        ssm_dt_bias[l])
        if xb is None:
            gact, xb = _gates(xf, w_t, l, bias_row, alog_row, True)
        else:
            gact, = _gates(xf, w_t, l, bias_row, alog_row, False)
        y_ml = _mlstm(_proj(xb, w_t, l, 0, ML_MAIN), gact, ml_norm_w[l], batch, seq)
        y_gdn = _gdn(_proj(xb, w_t, l, ML_COLS, GDN_MAIN), gact, gdn_conv_w[l], gdn_norm_w[l],
                     batch, seq)
        y_ssm = _ssd(_proj(xb, w_t, l, ML_COLS + GDN_COLS, SSM_MAIN), gact, ssm_conv_w[l],
                     ssm_conv_b[l], ssm_A_log[l], ssm_D[l], ssm_norm_w[l], batch, seq)
        y = _out_matmul(y_ml, y_gdn, y_ssm, w_o, l)
        xf, xb = _deepnorm(y, xf, ln_g[l], ln_b[l])
    return xf.reshape(batch, seq, d).astype(x.dtype)
```

```python
import functools

import jax
import jax.numpy as jnp
from jax import lax
from jax.experimental import pallas as pl
from jax.experimental.pallas import tpu as pltpu

F32 = jnp.float32
BF16 = jnp.bfloat16

D_MODEL = 4096
DEPTH = 2
CHUNK = 64
CONV_K = 4
ML_WIDTH = D_MODEL // 2
ML_HEADS = 4
ML_DV = ML_WIDTH // ML_HEADS
ML_DK = ML_DV // 2
GDN_WIDTH = D_MODEL // 2
GDN_HEAD_DIM = 128
GDN_HEADS = GDN_WIDTH // GDN_HEAD_DIM
SSM_WIDTH = D_MODEL
SSM_HEAD_DIM = 64
SSM_HEADS = SSM_WIDTH // SSM_HEAD_DIM
SSM_GROUPS = 8
SSM_STATE = 128
SSM_GROUP_WIDTH = SSM_WIDTH // SSM_GROUPS
SSM_GROUP_HEADS = SSM_HEADS // SSM_GROUPS
MIX_WIDTH = ML_WIDTH + GDN_WIDTH + SSM_WIDTH
DEEPNORM_ALPHA = (2 * DEPTH) ** 0.25
RMS_EPS = 1e-6
LN_EPS = 1e-5
NEG_BIG = -1e30

ML_MAIN = ML_HEADS * ML_DK * 2 + 3 * ML_WIDTH
ML_COLS = ML_MAIN + 2 * ML_HEADS
GDN_MAIN = 4 * GDN_WIDTH
GDN_COLS = GDN_MAIN + 2 * GDN_HEADS
SSM_MAIN = 2 * SSM_WIDTH + 2 * SSM_GROUPS * SSM_STATE
SSM_COLS = SSM_MAIN + SSM_HEADS

P_ML_Q = 0
P_ML_K = P_ML_Q + ML_HEADS * ML_DK
P_ML_V = P_ML_K + ML_HEADS * ML_DK
P_ML_O = P_ML_V + ML_WIDTH
P_ML_Z = P_ML_O + ML_WIDTH
P_GDN_Q = 0
P_GDN_K = P_GDN_Q + GDN_WIDTH
P_GDN_V = P_GDN_K + GDN_WIDTH
P_GDN_Z = P_GDN_V + GDN_WIDTH
P_SSM_Z = 0
P_SSM_X = P_SSM_Z + SSM_WIDTH
P_SSM_B = P_SSM_X + SSM_WIDTH
P_SSM_C = P_SSM_B + SSM_GROUPS * SSM_STATE

GATE_COLS = 128
G_ML_I = 0
G_ML_F = G_ML_I + ML_HEADS
G_GDN_BETA = G_ML_F + ML_HEADS
G_GDN_A = G_GDN_BETA + GDN_HEADS
G_SSM_DT = G_GDN_A + GDN_HEADS
G_END = G_SSM_DT + SSM_HEADS

HALO = 8
SEQ_BLOCK = 512
VMEM_LIMIT = 56 * 1024 * 1024


def _sigmoid(u):
    return 0.5 * jnp.tanh(0.5 * u) + 0.5


def _silu(u):
    h = 0.5 * u
    return h * jnp.tanh(h) + h


def _softplus(u):
    return jnp.maximum(u, 0.0) + jnp.log1p(jnp.exp(-jnp.abs(u)))


def _dot(a, b):
    return jnp.dot(a.astype(BF16), b.astype(BF16), preferred_element_type=F32)


def _dot_nt(a, b):
    return lax.dot_general(a.astype(BF16), b.astype(BF16), (((1,), (1,)), ((), ())),
                           preferred_element_type=F32)


def _dot_tn(a, b):
    return lax.dot_general(a.astype(BF16), b.astype(BF16), (((0,), (0,)), ((), ())),
                           preferred_element_type=F32)


def _split3(u):
    hi = u.astype(BF16)
    rest = u - hi.astype(F32)
    mid = rest.astype(BF16)
    lo = (rest - mid.astype(F32)).astype(BF16)
    return jnp.concatenate([hi, mid, lo], axis=1)


def _rms_rows(u, w_row):
    return u * lax.rsqrt(jnp.mean(u * u, axis=-1, keepdims=True) + RMS_EPS) * w_row


def _chunk_masks():
    ri = lax.broadcasted_iota(jnp.int32, (CHUNK, CHUNK), 0)
    ci = lax.broadcasted_iota(jnp.int32, (CHUNK, CHUNK), 1)
    return ci <= ri, ci < ri, ci == ri, ri <= ci


def _gate_column(g, col):
    lane = lax.broadcasted_iota(jnp.int32, g.shape, 1)
    return jnp.sum(jnp.where(lane == col, g, 0.0), axis=1, keepdims=True)


LANES = 128
PACK_ROWS = 1024
PACK_COLS = 512
PACK_SUB = 256


def _cast_kernel(w_ref, o_ref):
    def body(r, carry):
        rows = pl.ds(pl.multiple_of(r * PACK_SUB, PACK_SUB), PACK_SUB)
        o_ref[rows, :] = w_ref[rows, :].astype(BF16)
        return carry

    lax.fori_loop(0, PACK_ROWS // PACK_SUB, body, 0)


def _cast_bf16(w):
    layers, rows, cols = w.shape
    assert rows % PACK_ROWS == 0 and cols % PACK_COLS == 0
    blk = pl.BlockSpec((None, PACK_ROWS, PACK_COLS), lambda l, r, j: (l, r, j))
    return pl.pallas_call(
        _cast_kernel,
        grid=(layers, rows // PACK_ROWS, cols // PACK_COLS),
        in_specs=[blk],
        out_specs=blk,
        out_shape=jax.ShapeDtypeStruct((layers, rows, cols), BF16),
        compiler_params=pltpu.CompilerParams(
            dimension_semantics=("parallel", "parallel", "parallel"), vmem_limit_bytes=VMEM_LIMIT),
        name="cast_weights",
    )(w)


PROJ_TM = 512
PROJ_TN = 1024


def _proj_kernel(x_ref, wt_ref, o_ref, wbf_ref):
    @pl.when(pl.program_id(1) == 0)
    def _():
        def body(kc, carry):
            ks = pl.ds(pl.multiple_of(kc * LANES, LANES), LANES)
            for r in range(wt_ref.shape[0] // LANES):
                cols = slice(r * LANES, (r + 1) * LANES)
                wbf_ref[ks, cols] = wt_ref[cols, ks].astype(BF16).T
            return carry

        lax.fori_loop(0, wt_ref.shape[1] // LANES, body, 0)

    o_ref[...] = jnp.dot(x_ref[...], wbf_ref[...], preferred_element_type=F32)


def _proj(x, w_t, layer, start, width):
    m, kdim = x.shape
    tm, tn = min(PROJ_TM, m), PROJ_TN
    assert start % 8 == 0 and width % tn == 0 and m % tm == 0
    return pl.pallas_call(
        _proj_kernel,
        grid=(width // tn, m // tm),
        in_specs=[pl.BlockSpec((tm, kdim), lambda j, i: (i, 0)),
                  pl.BlockSpec((None, pl.Element(tn), pl.Element(kdim)),
                               lambda j, i: (layer, (start // 8 + j * (tn // 8)) * 8, 0))],
        out_specs=pl.BlockSpec((tm, tn), lambda j, i: (i, j)),
        out_shape=jax.ShapeDtypeStruct((m, width), F32),
        scratch_shapes=[pltpu.VMEM((kdim, tn), BF16)],
        compiler_params=pltpu.CompilerParams(
            dimension_semantics=("parallel", "arbitrary"), vmem_limit_bytes=VMEM_LIMIT),
        name="proj_matmul",
    )(x, w_t)


def _out_mm_kernel(yml_ref, ygdn_ref, yssm_ref, w_ref, o_ref, acc_ref, *, nk):
    k = pl.program_id(2)

    @pl.when(k == 0)
    def _():
        acc_ref[...] = jnp.dot(yml_ref[...], w_ref[...], preferred_element_type=F32)

    @pl.when(k == 1)
    def _():
        acc_ref[...] += jnp.dot(ygdn_ref[...], w_ref[...], preferred_element_type=F32)

    @pl.when(jnp.logical_and(k >= 2, k < nk - 1))
    def _():
        acc_ref[...] += jnp.dot(yssm_ref[...], w_ref[...], preferred_element_type=F32)

    @pl.when(k == nk - 1)
    def _():
        o_ref[...] = acc_ref[...] + jnp.dot(yssm_ref[...], w_ref[...], preferred_element_type=F32)


def _out_matmul(y_ml, y_gdn, y_ssm, w, layer, tm=1024, tn=1024):
    m = y_ml.shape[0]
    n = w.shape[2]
    tk = ML_WIDTH
    assert y_ml.shape[1] == tk and y_gdn.shape[1] == tk and y_ssm.shape[1] % tk == 0
    nk = w.shape[1] // tk
    tm, tn = min(tm, m), min(tn, n)
    return pl.pallas_call(
        functools.partial(_out_mm_kernel, nk=nk),
        grid=(n // tn, m // tm, nk),
        in_specs=[pl.BlockSpec((tm, tk), lambda j, i, k: (i, 0)),
                  pl.BlockSpec((tm, tk), lambda j, i, k: (i, 0)),
                  pl.BlockSpec((tm, tk), lambda j, i, k: (i, jnp.maximum(k - 2, 0))),
                  pl.BlockSpec((None, tk, tn), lambda j, i, k: (layer, k, j))],
        out_specs=pl.BlockSpec((tm, tn), lambda j, i, k: (i, j)),
        out_shape=jax.ShapeDtypeStruct((m, n), F32),
        scratch_shapes=[pltpu.VMEM((tm, tn), F32)],
        compiler_params=pltpu.CompilerParams(
            dimension_semantics=("parallel", "parallel", "arbitrary"),
            vmem_limit_bytes=VMEM_LIMIT),
        name="out_matmul",
    )(y_ml, y_gdn, y_ssm, w)


def _gates_kernel(x_ref, wml_ref, wgdn_ref, wssm_ref, bias_ref, alog_ref, o_ref, *maybe_xb_ref):
    pad = jnp.zeros((GATE_COLS - G_END, x_ref.shape[1]), F32)
    w_gate = jnp.concatenate([wml_ref[...], wgdn_ref[...], wssm_ref[...], pad], axis=0)
    x = x_ref[...]
    x_hi = x.astype(BF16)
    x_lo = (x - x_hi.astype(F32)).astype(BF16)
    w_hi = w_gate.astype(BF16)
    w_lo = (w_gate - w_hi.astype(F32)).astype(BF16)
    for xb_ref in maybe_xb_ref:
        xb_ref[...] = x_hi
    raw = _dot_nt(x_hi, w_hi) + (_dot_nt(x_hi, w_lo) + _dot_nt(x_lo, w_hi))
    col = lax.broadcasted_iota(jnp.int32, raw.shape, 1)
    u = raw + bias_ref[...]
    sp = _softplus(u)
    out = jnp.where(col < G_ML_F, u,
          jnp.where(col < G_GDN_BETA, u - sp,
          jnp.where(col < G_GDN_A, _sigmoid(u),
          jnp.where(col < G_SSM_DT, -jnp.exp(alog_ref[...]) * sp,
                    sp))))
    o_ref[...] = out


def _gates(x, w_t, layer, bias_row, alog_row, emit_bf16, tm=512):
    m, kdim = x.shape
    tm = min(tm, m)
    gate_rows = ((ML_MAIN, 2 * ML_HEADS), (ML_COLS + GDN_MAIN, 2 * GDN_HEADS),
                 (ML_COLS + GDN_COLS + SSM_MAIN, SSM_HEADS))
    assert all(s % 8 == 0 and n % 8 == 0 for s, n in gate_rows)
    wblk = lambda s, n: pl.BlockSpec((None, pl.Element(n), pl.Element(kdim)), lambda i: (layer, s, 0))
    row = pl.BlockSpec((1, GATE_COLS), lambda i: (0, 0))
    out_specs = [pl.BlockSpec((tm, GATE_COLS), lambda i: (i, 0))]
    out_shape = [jax.ShapeDtypeStruct((m, GATE_COLS), F32)]
    if emit_bf16:
        out_specs.append(pl.BlockSpec((tm, kdim), lambda i: (i, 0)))
        out_shape.append(jax.ShapeDtypeStruct((m, kdim), BF16))
    return pl.pallas_call(
        _gates_kernel,
        grid=(m // tm,),
        in_specs=[pl.BlockSpec((tm, kdim), lambda i: (i, 0))] + [wblk(s, n) for s, n in gate_rows] + [row, row],
        out_specs=out_specs,
        out_shape=out_shape,
        compiler_params=pltpu.CompilerParams(dimension_semantics=("parallel",),
                                             vmem_limit_bytes=VMEM_LIMIT),
        name="gate_proj",
    )(x, w_t, w_t, w_t, bias_row, alog_row)


ML_HB = 4


def _mlstm_kernel(q_ref, k_ref, v_ref, o_ref, z_ref, g_ref, nw_ref, out_ref,
                  c_ref, n_ref, m_ref, *, ts):
    head0 = pl.program_id(1) * ML_HB

    @pl.when(pl.program_id(2) == 0)
    def _():
        c_ref[...] = jnp.zeros_like(c_ref)
        n_ref[...] = jnp.zeros_like(n_ref)
        m_ref[...] = jnp.zeros_like(m_ref)

    causal, _, eye, upper = _chunk_masks()

    def chunk(c, carry):
        r0 = pl.multiple_of(c * CHUNK, CHUNK)
        rows = pl.ds(r0, CHUNK)
        g = g_ref[rows, :]
        heads = range(ML_HB)
        kl = [slice(hh * ML_DK, (hh + 1) * ML_DK) for hh in heads]
        vl = [slice(hh * ML_DV, (hh + 1) * ML_DV) for hh in heads]
        qf = [q_ref[rows, kl[hh]] * (ML_DK ** -0.5) for hh in heads]
        q = [qf[hh].astype(BF16) for hh in heads]
        k = [k_ref[rows, kl[hh]] for hh in heads]
        v = [v_ref[rows, vl[hh]].astype(BF16) for hh in heads]
        qk = [_dot_nt(q[hh], k[hh]) for hh in heads]
        q_c = [_dot(q[hh], c_ref[hh]) for hh in heads]
        li_c = [_gate_column(g, G_ML_I + head0 + hh) for hh in heads]
        lf_c = [_gate_column(g, G_ML_F + head0 + hh) for hh in heads]
        li_r = [jnp.sum(jnp.where(eye, li_c[hh], 0.0), axis=0, keepdims=True) for hh in heads]
        lf_r = [jnp.sum(jnp.where(eye, lf_c[hh], 0.0), axis=0, keepdims=True) for hh in heads]
        bcum_c = [jnp.sum(jnp.where(causal, lf_r[hh], 0.0), axis=1, keepdims=True) for hh in heads]
        bcum_r = [jnp.sum(jnp.where(upper, lf_c[hh], 0.0), axis=0, keepdims=True) for hh in heads]
        m_prev = [m_ref[hh] for hh in heads]
        dlog = [jnp.where(causal, bcum_c[hh] - bcum_r[hh] + li_r[hh], NEG_BIG) for hh in heads]
        inter = [bcum_c[hh] + m_prev[hh] for hh in heads]
        m_t = [jnp.maximum(inter[hh], jnp.max(dlog[hh], axis=1, keepdims=True)) for hh in heads]
        w_inter = [jnp.exp(inter[hh] - m_t[hh]) for hh in heads]
        sc = [qk[hh] * jnp.exp(dlog[hh] - m_t[hh]) for hh in heads]
        num = [_dot(sc[hh], v[hh]) + w_inter[hh] * q_c[hh] for hh in heads]
        den = [jnp.sum(sc[hh], axis=1, keepdims=True)
               + w_inter[hh] * jnp.sum(qf[hh] * n_ref[hh], axis=1, keepdims=True)
               for hh in heads]
        inv = [1.0 / jnp.maximum(jnp.abs(den[hh]), jnp.exp(-m_t[hh])) for hh in heads]
        hval = [num[hh] * inv[hh] for hh in heads]
        for hh in heads:
            b_last = bcum_c[hh][CHUNK - 1:CHUNK, :]
            gs_c = b_last - bcum_c[hh] + li_c[hh]
            gs_r = b_last - bcum_r[hh] + li_r[hh]
            m_new = jnp.maximum(b_last + m_prev[hh], jnp.max(gs_r, axis=1, keepdims=True))
            dec = jnp.exp(b_last + m_prev[hh] - m_new)
            kw = k[hh] * jnp.exp(gs_c - m_new)
            c_ref[hh] = dec * c_ref[hh] + _dot_tn(kw, v[hh])
            n_ref[hh] = dec * n_ref[hh] + jnp.sum(kw, axis=0, keepdims=True)
            m_ref[hh] = m_new
        for hh in heads:
            y = _rms_rows(hval[hh], nw_ref[:, vl[hh]])
            y = y * _sigmoid(o_ref[rows, vl[hh]]) * _silu(z_ref[rows, vl[hh]])
            out_ref[rows, vl[hh]] = y.astype(out_ref.dtype)
        return carry

    lax.fori_loop(0, ts // CHUNK, chunk, 0)


def _mlstm(proj, gact, norm_w, batch, seq):
    ts = min(SEQ_BLOCK, seq)
    ns = seq // ts
    kw_, vw = ML_HB * ML_DK, ML_HB * ML_DV
    row = lambda b, h, s: b * ns + s
    qk_blk = lambda off: pl.BlockSpec((ts, kw_), lambda b, h, s: (row(b, h, s), off // kw_ + h))
    v_blk = lambda off: pl.BlockSpec((ts, vw), lambda b, h, s: (row(b, h, s), off // vw + h))
    return pl.pallas_call(
        functools.partial(_mlstm_kernel, ts=ts),
        grid=(batch, ML_HEADS // ML_HB, ns),
        in_specs=[qk_blk(P_ML_Q), qk_blk(P_ML_K), v_blk(P_ML_V), v_blk(P_ML_O), v_blk(P_ML_Z),
                  pl.BlockSpec((ts, GATE_COLS), lambda b, h, s: (row(b, h, s), 0)),
                  pl.BlockSpec((1, vw), lambda b, h, s: (0, h))],
        out_specs=pl.BlockSpec((ts, vw), lambda b, h, s: (row(b, h, s), h)),
        out_shape=jax.ShapeDtypeStruct((batch * seq, ML_WIDTH), BF16),
        scratch_shapes=[pltpu.VMEM((ML_HB, ML_DK, ML_DV), F32), pltpu.VMEM((ML_HB, 1, ML_DK), F32),
                        pltpu.VMEM((ML_HB, 1, 1), F32)],
        compiler_params=pltpu.CompilerParams(
            dimension_semantics=("parallel", "parallel", "arbitrary"),
            vmem_limit_bytes=VMEM_LIMIT),
        name="mlstm_mixer",
    )(proj, proj, proj, proj, proj, gact, norm_w.reshape(1, ML_WIDTH))


def _conv_block(dst_ref, cur_ref, halo_ref, w_ref, first, ts, piece, post):
    back = CONV_K - 1
    for lane0 in range(0, cur_ref.shape[1], piece):
        lanes = slice(lane0, lane0 + piece)
        w = [w_ref[tap:tap + 1, lanes] for tap in range(CONV_K)]
        for r0 in range(0, ts, CHUNK):
            if r0 == 0:
                win = jnp.concatenate([jnp.where(first, 0.0, halo_ref[:, lanes]), cur_ref[0:CHUNK, lanes]],
                                      axis=0)
                taps = [win[HALO - back + tap:HALO - back + tap + CHUNK, :] for tap in range(CONV_K)]
            else:
                taps = [cur_ref[pl.ds(r0 - back + tap, CHUNK), lanes] for tap in range(CONV_K)]
            acc = w[0] * taps[0]
            for tap in range(1, CONV_K):
                acc = acc + w[tap] * taps[tap]
            dst_ref[r0:r0 + CHUNK, lanes] = post(acc, lanes)


GDN_HB = 16


def _gdn_kernel(q_ref, k_ref, v_ref, qh_ref, kh_ref, vh_ref, z_ref, g_ref,
                wq_ref, wk_ref, wv_ref, nw_ref, out_ref,
                s_ref, qs_ref, ks_ref, vs_ref, *, ts):
    head0 = pl.program_id(1) * GDN_HB
    first = pl.program_id(2) == 0
    d = GDN_HEAD_DIM

    @pl.when(first)
    def _():
        s_ref[...] = jnp.zeros_like(s_ref)

    def l2n(u):
        return u * lax.rsqrt(jnp.sum(u * u, axis=-1, keepdims=True) + RMS_EPS)

    _conv_block(ks_ref, k_ref, kh_ref, wk_ref, first, ts, d, lambda u, _: l2n(_silu(u)))
    _conv_block(qs_ref, q_ref, qh_ref, wq_ref, first, ts, d, lambda u, _: l2n(_silu(u)) * (d ** -0.5))
    _conv_block(vs_ref, v_ref, vh_ref, wv_ref, first, ts, d, lambda u, _: _silu(u))

    causal, strict, eye, upper = _chunk_masks()
    eye_f = eye.astype(F32)
    nw = nw_ref[...]

    def chunk(c, carry):
        r0 = pl.multiple_of(c * CHUNK, CHUNK)
        rows = pl.ds(r0, CHUNK)
        g = g_ref[rows, :]
        heads = range(GDN_HB)
        lanes = [slice(hh * d, (hh + 1) * d) for hh in heads]
        k = [ks_ref[rows, lanes[hh]] for hh in heads]
        q = [qs_ref[rows, lanes[hh]] for hh in heads]
        kq = [jnp.concatenate([k[hh], q[hh]], axis=0).astype(BF16) for hh in heads]
        kk_qk = [_dot_nt(kq[hh], kq[hh][:CHUNK]) for hh in heads]
        kq_s = [_dot(kq[hh], s_ref[hh]) for hh in heads]
        b_c = [_gate_column(g, G_GDN_BETA + head0 + hh) for hh in heads]
        g_c = [_gate_column(g, G_GDN_A + head0 + hh) for hh in heads]
        g_r = [jnp.sum(jnp.where(eye, g_c[hh], 0.0), axis=0, keepdims=True) for hh in heads]
        gc_c = [jnp.sum(jnp.where(causal, g_r[hh], 0.0), axis=1, keepdims=True) for hh in heads]
        gc_r = [jnp.sum(jnp.where(upper, g_c[hh], 0.0), axis=0, keepdims=True) for hh in heads]
        decay = [jnp.exp(jnp.where(causal, gc_c[hh] - gc_r[hh], NEG_BIG)) for hh in heads]
        a = [jnp.where(strict, b_c[hh] * kk_qk[hh][:CHUNK] * decay[hh], 0.0) for hh in heads]
        tinv = [eye_f - a[hh] for hh in heads]
        apow = [_dot(a[hh], a[hh]) for hh in heads]
        for step in range(4):
            both = [_dot(jnp.concatenate([apow[hh], tinv[hh]], axis=0), apow[hh]) for hh in heads]
            apow = [both[hh][:CHUNK] for hh in heads]
            tinv = [tinv[hh] + both[hh][CHUNK:] for hh in heads]
        tinv = [tinv[hh] + _dot(tinv[hh], apow[hh]) for hh in heads]
        v = [vs_ref[rows, lanes[hh]] for hh in heads]
        eg_c = [jnp.exp(gc_c[hh]) for hh in heads]
        v_new = [_dot(tinv[hh], b_c[hh] * (v[hh] - eg_c[hh] * kq_s[hh][:CHUNK])) for hh in heads]
        o = [eg_c[hh] * kq_s[hh][CHUNK:] + _dot(kk_qk[hh][CHUNK:] * decay[hh], v_new[hh]) for hh in heads]
        for hh in heads:
            g_last = gc_c[hh][CHUNK - 1:CHUNK, :]
            s_ref[hh] = (s_ref[hh] * jnp.exp(g_last)
                         + _dot_tn(k[hh] * jnp.exp(g_last - gc_c[hh]), v_new[hh]))
        for hh in heads:
            y = _rms_rows(o[hh], nw) * _silu(z_ref[rows, lanes[hh]])
            out_ref[rows, lanes[hh]] = y.astype(out_ref.dtype)
        return carry

    lax.fori_loop(0, ts // CHUNK, chunk, 0)


def _halo_row(b, s, ns, ts):
    return jnp.maximum((b * ns + s) * (ts // HALO) - 1, 0)


def _gdn(proj, gact, conv_w, norm_w, batch, seq):
    ts = min(SEQ_BLOCK, seq)
    ns = seq // ts
    d = GDN_HEAD_DIM
    w = GDN_HB * d
    row = lambda b, h, s: b * ns + s
    cur = lambda off: pl.BlockSpec((ts, w), lambda b, h, s: (row(b, h, s), off // w + h))
    halo = lambda off: pl.BlockSpec((HALO, w), lambda b, h, s: (_halo_row(b, s, ns, ts), off // w + h))
    cw = lambda off: pl.BlockSpec((CONV_K, w), lambda b, h, s: (0, off // w + h))
    return pl.pallas_call(
        functools.partial(_gdn_kernel, ts=ts),
        grid=(batch, GDN_HEADS // GDN_HB, ns),
        in_specs=[cur(P_GDN_Q), cur(P_GDN_K), cur(P_GDN_V),
                  halo(P_GDN_Q), halo(P_GDN_K), halo(P_GDN_V),
                  cur(P_GDN_Z),
                  pl.BlockSpec((ts, GATE_COLS), lambda b, h, s: (row(b, h, s), 0)),
                  cw(0), cw(GDN_WIDTH), cw(2 * GDN_WIDTH),
                  pl.BlockSpec((1, d), lambda b, h, s: (0, 0))],
        out_specs=pl.BlockSpec((ts, w), lambda b, h, s: (row(b, h, s), h)),
        out_shape=jax.ShapeDtypeStruct((batch * seq, GDN_WIDTH), BF16),
        scratch_shapes=[pltpu.VMEM((GDN_HB, d, d), F32)] + [pltpu.VMEM((ts, w), F32)] * 3,
        compiler_params=pltpu.CompilerParams(
            dimension_semantics=("parallel", "parallel", "arbitrary"),
            vmem_limit_bytes=VMEM_LIMIT),
        name="gdn_mixer",
    )(proj, proj, proj, proj, proj, proj, proj, gact, conv_w, conv_w, conv_w, norm_w.reshape(1, d))


SSD_GB = 4
SSD_SEQ_BLOCK = SEQ_BLOCK


def _ssd_kernel(z_ref, x_ref, b_ref, c_ref, xh_ref, bh_ref, ch_ref, g_ref,
                wx_ref, wb_ref, wc_ref, bx_ref, bb_ref, bc_ref, alog_ref, dskip_ref, nw_ref, out_ref,
                st_ref, bufx_ref, bufb_ref, bufc_ref, sel_ref, *, ts):
    group0 = pl.program_id(1) * SSD_GB
    first = pl.program_id(2) == 0
    gw = SSM_GROUP_WIDTH
    hd = SSM_HEAD_DIM
    n = SSM_STATE
    pairs = gw // (2 * hd)

    @pl.when(first)
    def _():
        st_ref[...] = jnp.zeros_like(st_ref)

    _conv_block(bufb_ref, b_ref, bh_ref, wb_ref, first, ts, n, lambda u, l: _silu(u + bb_ref[:, l]))
    _conv_block(bufc_ref, c_ref, ch_ref, wc_ref, first, ts, n, lambda u, l: _silu(u + bc_ref[:, l]))
    _conv_block(bufx_ref, x_ref, xh_ref, wx_ref, first, ts, 2 * n, lambda u, l: _silu(u + bx_ref[:, l]))

    sel_r = lax.broadcasted_iota(jnp.int32, (3 * GATE_COLS, gw), 0) % GATE_COLS
    sel_c = lax.broadcasted_iota(jnp.int32, (3 * GATE_COLS, gw), 1)
    for gg in range(SSD_GB):
        sel_ref[gg] = (sel_r == G_SSM_DT + (group0 + gg) * SSM_GROUP_HEADS + sel_c // hd).astype(BF16)

    ri = lax.broadcasted_iota(jnp.int32, (CHUNK, gw), 0)
    cm = lax.broadcasted_iota(jnp.int32, (CHUNK, gw), 1) % hd
    causal_w = cm <= ri
    upper_w = ri <= cm
    tri_b = _chunk_masks()[0].astype(BF16)
    lane_lo = lax.broadcasted_iota(jnp.int32, (CHUNK, 2 * hd), 1) < hd

    def chunk(c, carry):
        r0 = pl.multiple_of(c * CHUNK, CHUNK)
        rows = pl.ds(r0, CHUNK)
        g = g_ref[rows, :]
        groups = range(SSD_GB)
        xl = [slice(gg * gw, (gg + 1) * gw) for gg in groups]
        nl = [slice(gg * n, (gg + 1) * n) for gg in groups]
        g3 = _split3(g)
        cum3 = jnp.dot(tri_b, g3, preferred_element_type=F32)
        gcum = cum3[:, :GATE_COLS] + cum3[:, GATE_COLS:2 * GATE_COLS] + cum3[:, 2 * GATE_COLS:]
        both3 = jnp.concatenate([g3, _split3(gcum)], axis=0)
        expanded = [jnp.dot(both3, sel_ref[gg], preferred_element_type=F32) for gg in groups]
        neg_a = [-jnp.exp(alog_ref[:, xl[gg]]) for gg in groups]
        dt = [expanded[gg][:CHUNK] for gg in groups]
        a = [dt[gg] * neg_a[gg] for gg in groups]
        acs = [expanded[gg][CHUNK:] * neg_a[gg] for gg in groups]
        bm = [bufb_ref[rows, nl[gg]] for gg in groups]
        cmat = [bufc_ref[rows, nl[gg]] for gg in groups]
        cb2 = [_dot_nt(cmat[gg], jnp.concatenate([bm[gg], bm[gg]], axis=0)) for gg in groups]
        y_off = [_dot(cmat[gg], st_ref[gg]) for gg in groups]
        x = [bufx_ref[rows, xl[gg]] for gg in groups]
        xdt = [x[gg] * dt[gg] for gg in groups]
        acs_r = [jnp.sum(jnp.where(upper_w, a[gg], 0.0), axis=0, keepdims=True) for gg in groups]
        lm = [jnp.exp(jnp.where(causal_w, acs[gg] - acs_r[gg], NEG_BIG)) for gg in groups]
        mm = [(jnp.concatenate([cb2[gg]] * pairs, axis=1) * lm[gg]).astype(BF16) for gg in groups]
        y_diag = []
        for gg in groups:
            parts = []
            for p in range(pairs):
                xp = xdt[gg][:, p * 2 * hd:(p + 1) * 2 * hd]
                rhs = jnp.concatenate([jnp.where(lane_lo, xp, 0.0), jnp.where(lane_lo, 0.0, xp)], axis=0)
                parts.append(_dot(mm[gg][:, p * 2 * hd:(p + 1) * 2 * hd], rhs))
            y_diag.append(jnp.concatenate(parts, axis=1))
        for gg in groups:
            a_last = acs[gg][CHUNK - 1:CHUNK, :]
            st_ref[gg] = (st_ref[gg] * jnp.exp(a_last)
                          + _dot_tn(bm[gg], xdt[gg] * jnp.exp(a_last - acs[gg])))
        for gg in groups:
            y = y_diag[gg] + y_off[gg] * jnp.exp(acs[gg]) + dskip_ref[:, xl[gg]] * x[gg]
            y = y * _silu(z_ref[rows, xl[gg]])
            out_ref[rows, xl[gg]] = _rms_rows(y, nw_ref[:, xl[gg]]).astype(out_ref.dtype)
        return carry

    lax.fori_loop(0, ts // CHUNK, chunk, 0)


def _ssd(proj, gact, conv_w, conv_b, a_log, dskip, norm_w, batch, seq):
    ts = min(SSD_SEQ_BLOCK, seq)
    ns = seq // ts
    gw, n = SSD_GB * SSM_GROUP_WIDTH, SSD_GB * SSM_STATE
    row = lambda b, g, s: b * ns + s
    cur = lambda off, w: pl.BlockSpec((ts, w), lambda b, g, s: (row(b, g, s), off // w + g))
    halo = lambda off, w: pl.BlockSpec((HALO, w), lambda b, g, s: (_halo_row(b, s, ns, ts), off // w + g))
    par = lambda rows_, off, w: pl.BlockSpec((rows_, w), lambda b, g, s: (0, off // w + g))
    expand = lambda p: jnp.repeat(p, SSM_HEAD_DIM).reshape(1, SSM_WIDTH)
    conv_b = conv_b.reshape(1, -1)
    return pl.pallas_call(
        functools.partial(_ssd_kernel, ts=ts),
        grid=(batch, SSM_GROUPS // SSD_GB, ns),
        in_specs=[cur(P_SSM_Z, gw), cur(P_SSM_X, gw), cur(P_SSM_B, n), cur(P_SSM_C, n),
                  halo(P_SSM_X, gw), halo(P_SSM_B, n), halo(P_SSM_C, n),
                  pl.BlockSpec((ts, GATE_COLS), lambda b, g, s: (row(b, g, s), 0)),
                  par(CONV_K, 0, gw), par(CONV_K, P_SSM_B - P_SSM_X, n), par(CONV_K, P_SSM_C - P_SSM_X, n),
                  par(1, 0, gw), par(1, P_SSM_B - P_SSM_X, n), par(1, P_SSM_C - P_SSM_X, n),
                  par(1, 0, gw), par(1, 0, gw), par(1, 0, gw)],
        out_specs=pl.BlockSpec((ts, gw), lambda b, g, s: (row(b, g, s), g)),
        out_shape=jax.ShapeDtypeStruct((batch * seq, SSM_WIDTH), BF16),
        scratch_shapes=[pltpu.VMEM((SSD_GB, SSM_STATE, SSM_GROUP_WIDTH), F32),
                        pltpu.VMEM((ts, gw), F32), pltpu.VMEM((ts, n), F32), pltpu.VMEM((ts, n), F32),
                        pltpu.VMEM((SSD_GB, 3 * GATE_COLS, SSM_GROUP_WIDTH), BF16)],
        compiler_params=pltpu.CompilerParams(
            dimension_semantics=("parallel", "parallel", "arbitrary"),
            vmem_limit_bytes=VMEM_LIMIT),
        name="ssd_mixer",
    )(proj, proj, proj, proj, proj, proj, proj, gact, conv_w, conv_w, conv_w, conv_b, conv_b, conv_b,
      expand(a_log), expand(dskip), norm_w.reshape(1, SSM_WIDTH))


def _deepnorm_kernel(y_ref, x_ref, g_ref, b_ref, o_ref, obf_ref):
    u = DEEPNORM_ALPHA * x_ref[...] + y_ref[...]
    mu = jnp.mean(u, axis=-1, keepdims=True)
    var = jnp.mean(jnp.square(u - mu), axis=-1, keepdims=True)
    out = (u - mu) * lax.rsqrt(var + LN_EPS) * g_ref[...] + b_ref[...]
    o_ref[...] = out
    obf_ref[...] = out.astype(BF16)


def _deepnorm(y, x, ln_g, ln_b, tm=256):
    m, d = x.shape
    tm = min(tm, m)
    blk = pl.BlockSpec((tm, d), lambda i: (i, 0))
    vec = pl.BlockSpec((1, d), lambda i: (0, 0))
    return pl.pallas_call(
        _deepnorm_kernel,
        grid=(m // tm,),
        in_specs=[blk, blk, vec, vec],
        out_specs=[blk, blk],
        out_shape=[jax.ShapeDtypeStruct((m, d), F32), jax.ShapeDtypeStruct((m, d), BF16)],
        compiler_params=pltpu.CompilerParams(dimension_semantics=("parallel",),
                                             vmem_limit_bytes=VMEM_LIMIT),
        name="deepnorm_ln",
    )(y, x, ln_g.reshape(1, d), ln_b.reshape(1, d))


def _gate_rows(ml_i_bias, ml_f_bias, gdn_dt_bias, gdn_a_log, ssm_dt_bias):
    zeros = lambda n: jnp.zeros((n,), F32)
    bias = jnp.concatenate([ml_i_bias, ml_f_bias, zeros(GDN_HEADS), gdn_dt_bias, ssm_dt_bias,
                            zeros(GATE_COLS - G_END)])
    alog = jnp.concatenate([zeros(G_GDN_A), gdn_a_log, zeros(GATE_COLS - G_SSM_DT)])
    return bias.reshape(1, GATE_COLS), alog.reshape(1, GATE_COLS)


def kernel(x, w_in, w_out, ml_i_bias, ml_f_bias, ml_norm_w, gdn_conv_w, gdn_A_log, gdn_dt_bias,
           gdn_norm_w, ssm_conv_w, ssm_conv_b, ssm_A_log, ssm_dt_bias, ssm_D, ssm_norm_w, ln_g, ln_b):
    batch, seq, d = x.shape
    xf = x.reshape(batch * seq, d).astype(F32)
    xb = None
    w_t = jnp.swapaxes(w_in.astype(F32), 1, 2)
    w_o = _cast_bf16(w_out.astype(F32))
    for l in range(w_in.shape[0]):
        bias_row, alog_row = _gate_rows(ml_i_bias[l], ml_f_bias[l], gdn_dt_bias[l], gdn_A_log[l],
                                        ssm_dt_bias[l])
        if xb is None:
            gact, xb = _gates(xf, w_t, l, bias_row, alog_row, True)
        else:
            gact, = _gates(xf, w_t, l, bias_row, alog_row, False)
        y_ml = _mlstm(_proj(xb, w_t, l, 0, ML_MAIN), gact, ml_norm_w[l], batch, seq)
        y_gdn = _gdn(_proj(xb, w_t, l, ML_COLS, GDN_MAIN), gact, gdn_conv_w[l], gdn_norm_w[l],
                     batch, seq)
        y_ssm = _ssd(_proj(xb, w_t, l, ML_COLS + GDN_COLS, SSM_MAIN), gact, ssm_conv_w[l],
                     ssm_conv_b[l], ssm_A_log[l], ssm_D[l], ssm_norm_w[l], batch, seq)
        y = _out_matmul(y_ml, y_gdn, y_ssm, w_o, l)
        xf, xb = _deepnorm(y, xf, ln_g[l], ln_b[l])
    return xf.reshape(batch, seq, d).astype(x.dtype)
```

```python
import functools

import jax
import jax.numpy as jnp
from jax import lax
from jax.experimental import pallas as pl
from jax.experimental.pallas import tpu as pltpu

F32 = jnp.float32
BF16 = jnp.bfloat16

D_MODEL = 4096
DEPTH = 2
CHUNK = 64
CONV_K = 4
ML_WIDTH = D_MODEL // 2
ML_HEADS = 4
ML_DV = ML_WIDTH // ML_HEADS
ML_DK = ML_DV // 2
GDN_WIDTH = D_MODEL // 2
GDN_HEAD_DIM = 128
GDN_HEADS = GDN_WIDTH // GDN_HEAD_DIM
SSM_WIDTH = D_MODEL
SSM_HEAD_DIM = 64
SSM_HEADS = SSM_WIDTH // SSM_HEAD_DIM
SSM_GROUPS = 8
SSM_STATE = 128
SSM_GROUP_WIDTH = SSM_WIDTH // SSM_GROUPS
SSM_GROUP_HEADS = SSM_HEADS // SSM_GROUPS
MIX_WIDTH = ML_WIDTH + GDN_WIDTH + SSM_WIDTH
DEEPNORM_ALPHA = (2 * DEPTH) ** 0.25
RMS_EPS = 1e-6
LN_EPS = 1e-5
NEG_BIG = -1e30

ML_MAIN = ML_HEADS * ML_DK * 2 + 3 * ML_WIDTH
ML_COLS = ML_MAIN + 2 * ML_HEADS
GDN_MAIN = 4 * GDN_WIDTH
GDN_COLS = GDN_MAIN + 2 * GDN_HEADS
SSM_MAIN = 2 * SSM_WIDTH + 2 * SSM_GROUPS * SSM_STATE
SSM_COLS = SSM_MAIN + SSM_HEADS

P_ML_Q = 0
P_ML_K = P_ML_Q + ML_HEADS * ML_DK
P_ML_V = P_ML_K + ML_HEADS * ML_DK
P_ML_O = P_ML_V + ML_WIDTH
P_ML_Z = P_ML_O + ML_WIDTH
P_GDN_Q = 0
P_GDN_K = P_GDN_Q + GDN_WIDTH
P_GDN_V = P_GDN_K + GDN_WIDTH
P_GDN_Z = P_GDN_V + GDN_WIDTH
P_SSM_Z = 0
P_SSM_X = P_SSM_Z + SSM_WIDTH
P_SSM_B = P_SSM_X + SSM_WIDTH
P_SSM_C = P_SSM_B + SSM_GROUPS * SSM_STATE

GATE_COLS = 128
G_ML_I = 0
G_ML_F = G_ML_I + ML_HEADS
G_GDN_BETA = G_ML_F + ML_HEADS
G_GDN_A = G_GDN_BETA + GDN_HEADS
G_SSM_DT = G_GDN_A + GDN_HEADS
G_END = G_SSM_DT + SSM_HEADS

HALO = 8
SEQ_BLOCK = 512
VMEM_LIMIT = 56 * 1024 * 1024


def _sigmoid(u):
    return 0.5 * jnp.tanh(0.5 * u) + 0.5


def _silu(u):
    h = 0.5 * u
    return h * jnp.tanh(h) + h


def _softplus(u):
    return jnp.maximum(u, 0.0) + jnp.log1p(jnp.exp(-jnp.abs(u)))


def _dot(a, b):
    return jnp.dot(a.astype(BF16), b.astype(BF16), preferred_element_type=F32)


def _dot_nt(a, b):
    return lax.dot_general(a.astype(BF16), b.astype(BF16), (((1,), (1,)), ((), ())),
                           preferred_element_type=F32)


def _dot_tn(a, b):
    return lax.dot_general(a.astype(BF16), b.astype(BF16), (((0,), (0,)), ((), ())),
                           preferred_element_type=F32)


def _split3(u):
    hi = u.astype(BF16)
    rest = u - hi.astype(F32)
    mid = rest.astype(BF16)
    lo = (rest - mid.astype(F32)).astype(BF16)
    return jnp.concatenate([hi, mid, lo], axis=1)


def _rms_rows(u, w_row):
    return u * lax.rsqrt(jnp.mean(u * u, axis=-1, keepdims=True) + RMS_EPS) * w_row


def _chunk_masks():
    ri = lax.broadcasted_iota(jnp.int32, (CHUNK, CHUNK), 0)
    ci = lax.broadcasted_iota(jnp.int32, (CHUNK, CHUNK), 1)
    return ci <= ri, ci < ri, ci == ri, ri <= ci


def _gate_column(g, col):
    lane = lax.broadcasted_iota(jnp.int32, g.shape, 1)
    return jnp.sum(jnp.where(lane == col, g, 0.0), axis=1, keepdims=True)


LANES = 128
PACK_ROWS = 1024
PACK_COLS = 512
PACK_SUB = 256


def _cast_kernel(w_ref, o_ref):
    def body(r, carry):
        rows = pl.ds(pl.multiple_of(r * PACK_SUB, PACK_SUB), PACK_SUB)
        o_ref[rows, :] = w_ref[rows, :].astype(BF16)
        return carry

    lax.fori_loop(0, PACK_ROWS // PACK_SUB, body, 0)


def _cast_bf16(w):
    layers, rows, cols = w.shape
    assert rows % PACK_ROWS == 0 and cols % PACK_COLS == 0
    blk = pl.BlockSpec((None, PACK_ROWS, PACK_COLS), lambda l, r, j: (l, r, j))
    return pl.pallas_call(
        _cast_kernel,
        grid=(layers, rows // PACK_ROWS, cols // PACK_COLS),
        in_specs=[blk],
        out_specs=blk,
        out_shape=jax.ShapeDtypeStruct((layers, rows, cols), BF16),
        compiler_params=pltpu.CompilerParams(
            dimension_semantics=("parallel", "parallel", "parallel"), vmem_limit_bytes=VMEM_LIMIT),
        name="cast_weights",
    )(w)


PROJ_TM = 512
PROJ_TN = 1024


def _proj_kernel(x_ref, wt_ref, o_ref, wbf_ref):
    @pl.when(pl.program_id(1) == 0)
    def _():
        def body(r, carry):
            rows = pl.ds(pl.multiple_of(r * LANES, LANES), LANES)
            wbf_ref[rows, :] = wt_ref[rows, :].astype(BF16)
            return carry

        lax.fori_loop(0, wt_ref.shape[0] // LANES, body, 0)

    o_ref[...] = lax.dot_general(x_ref[...], wbf_ref[...], (((1,), (1,)), ((), ())),
                                 preferred_element_type=F32)


def _proj(x, w_t, layer, start, width):
    m, kdim = x.shape
    tm, tn = min(PROJ_TM, m), PROJ_TN
    assert start % 8 == 0 and width % tn == 0 and m % tm == 0
    return pl.pallas_call(
        _proj_kernel,
        grid=(width // tn, m // tm),
        in_specs=[pl.BlockSpec((tm, kdim), lambda j, i: (i, 0)),
                  pl.BlockSpec((None, pl.Element(tn), pl.Element(kdim)),
                               lambda j, i: (layer, (start // 8 + j * (tn // 8)) * 8, 0))],
        out_specs=pl.BlockSpec((tm, tn), lambda j, i: (i, j)),
        out_shape=jax.ShapeDtypeStruct((m, width), F32),
        scratch_shapes=[pltpu.VMEM((tn, kdim), BF16)],
        compiler_params=pltpu.CompilerParams(
            dimension_semantics=("parallel", "arbitrary"), vmem_limit_bytes=VMEM_LIMIT),
        name="proj_matmul",
    )(x, w_t)


def _out_mm_kernel(yml_ref, ygdn_ref, yssm_ref, w_ref, o_ref, acc_ref, *, nk):
    k = pl.program_id(2)

    @pl.when(k == 0)
    def _():
        acc_ref[...] = jnp.dot(yml_ref[...], w_ref[...], preferred_element_type=F32)

    @pl.when(k == 1)
    def _():
        acc_ref[...] += jnp.dot(ygdn_ref[...], w_ref[...], preferred_element_type=F32)

    @pl.when(jnp.logical_and(k >= 2, k < nk - 1))
    def _():
        acc_ref[...] += jnp.dot(yssm_ref[...], w_ref[...], preferred_element_type=F32)

    @pl.when(k == nk - 1)
    def _():
        o_ref[...] = acc_ref[...] + jnp.dot(yssm_ref[...], w_ref[...], preferred_element_type=F32)


def _out_matmul(y_ml, y_gdn, y_ssm, w, layer, tm=1024, tn=1024):
    m = y_ml.shape[0]
    n = w.shape[2]
    tk = ML_WIDTH
    assert y_ml.shape[1] == tk and y_gdn.shape[1] == tk and y_ssm.shape[1] % tk == 0
    nk = w.shape[1] // tk
    tm, tn = min(tm, m), min(tn, n)
    return pl.pallas_call(
        functools.partial(_out_mm_kernel, nk=nk),
        grid=(n // tn, m // tm, nk),
        in_specs=[pl.BlockSpec((tm, tk), lambda j, i, k: (i, 0)),
                  pl.BlockSpec((tm, tk), lambda j, i, k: (i, 0)),
                  pl.BlockSpec((tm, tk), lambda j, i, k: (i, jnp.maximum(k - 2, 0))),
                  pl.BlockSpec((None, tk, tn), lambda j, i, k: (layer, k, j))],
        out_specs=pl.BlockSpec((tm, tn), lambda j, i, k: (i, j)),
        out_shape=jax.ShapeDtypeStruct((m, n), F32),
        scratch_shapes=[pltpu.VMEM((tm, tn), F32)],
        compiler_params=pltpu.CompilerParams(
            dimension_semantics=("parallel", "parallel", "arbitrary"),
            vmem_limit_bytes=VMEM_LIMIT),
        name="out_matmul",
    )(y_ml, y_gdn, y_ssm, w)


def _gates_kernel(x_ref, wml_ref, wgdn_ref, wssm_ref, bias_ref, alog_ref, o_ref, *maybe_xb_ref):
    pad = jnp.zeros((GATE_COLS - G_END, x_ref.shape[1]), F32)
    w_gate = jnp.concatenate([wml_ref[...], wgdn_ref[...], wssm_ref[...], pad], axis=0)
    x = x_ref[...]
    x_hi = x.astype(BF16)
    x_lo = (x - x_hi.astype(F32)).astype(BF16)
    w_hi = w_gate.astype(BF16)
    w_lo = (w_gate - w_hi.astype(F32)).astype(BF16)
    for xb_ref in maybe_xb_ref:
        xb_ref[...] = x_hi
    raw = _dot_nt(x_hi, w_hi) + (_dot_nt(x_hi, w_lo) + _dot_nt(x_lo, w_hi))
    col = lax.broadcasted_iota(jnp.int32, raw.shape, 1)
    u = raw + bias_ref[...]
    sp = _softplus(u)
    out = jnp.where(col < G_ML_F, u,
          jnp.where(col < G_GDN_BETA, u - sp,
          jnp.where(col < G_GDN_A, _sigmoid(u),
          jnp.where(col < G_SSM_DT, -jnp.exp(alog_ref[...]) * sp,
                    sp))))
    o_ref[...] = out


def _gates(x, w_t, layer, bias_row, alog_row, emit_bf16, tm=512):
    m, kdim = x.shape
    tm = min(tm, m)
    gate_rows = ((ML_MAIN, 2 * ML_HEADS), (ML_COLS + GDN_MAIN, 2 * GDN_HEADS),
                 (ML_COLS + GDN_COLS + SSM_MAIN, SSM_HEADS))
    assert all(s % 8 == 0 and n % 8 == 0 for s, n in gate_rows)
    wblk = lambda s, n: pl.BlockSpec((None, pl.Element(n), pl.Element(kdim)), lambda i: (layer, s, 0))
    row = pl.BlockSpec((1, GATE_COLS), lambda i: (0, 0))
    out_specs = [pl.BlockSpec((tm, GATE_COLS), lambda i: (i, 0))]
    out_shape = [jax.ShapeDtypeStruct((m, GATE_COLS), F32)]
    if emit_bf16:
        out_specs.append(pl.BlockSpec((tm, kdim), lambda i: (i, 0)))
        out_shape.append(jax.ShapeDtypeStruct((m, kdim), BF16))
    return pl.pallas_call(
        _gates_kernel,
        grid=(m // tm,),
        in_specs=[pl.BlockSpec((tm, kdim), lambda i: (i, 0))] + [wblk(s, n) for s, n in gate_rows] + [row, row],
        out_specs=out_specs,
        out_shape=out_shape,
        compiler_params=pltpu.CompilerParams(dimension_semantics=("parallel",),
                                             vmem_limit_bytes=VMEM_LIMIT),
        name="gate_proj",
    )(x, w_t, w_t, w_t, bias_row, alog_row)


ML_HB = 4


def _mlstm_kernel(q_ref, k_ref, v_ref, o_ref, z_ref, g_ref, nw_ref, out_ref,
                  c_ref, n_ref, m_ref, *, ts):
    head0 = pl.program_id(1) * ML_HB

    @pl.when(pl.program_id(2) == 0)
    def _():
        c_ref[...] = jnp.zeros_like(c_ref)
        n_ref[...] = jnp.zeros_like(n_ref)
        m_ref[...] = jnp.zeros_like(m_ref)

    causal, _, eye, upper = _chunk_masks()

    def chunk(c, carry):
        r0 = pl.multiple_of(c * CHUNK, CHUNK)
        rows = pl.ds(r0, CHUNK)
        g = g_ref[rows, :]
        heads = range(ML_HB)
        kl = [slice(hh * ML_DK, (hh + 1) * ML_DK) for hh in heads]
        vl = [slice(hh * ML_DV, (hh + 1) * ML_DV) for hh in heads]
        qf = [q_ref[rows, kl[hh]] * (ML_DK ** -0.5) for hh in heads]
        q = [qf[hh].astype(BF16) for hh in heads]
        k = [k_ref[rows, kl[hh]] for hh in heads]
        v = [v_ref[rows, vl[hh]].astype(BF16) for hh in heads]
        qk = [_dot_nt(q[hh], k[hh]) for hh in heads]
        q_c = [_dot(q[hh], c_ref[hh]) for hh in heads]
        li_c = [_gate_column(g, G_ML_I + head0 + hh) for hh in heads]
        lf_c = [_gate_column(g, G_ML_F + head0 + hh) for hh in heads]
        li_r = [jnp.sum(jnp.where(eye, li_c[hh], 0.0), axis=0, keepdims=True) for hh in heads]
        lf_r = [jnp.sum(jnp.where(eye, lf_c[hh], 0.0), axis=0, keepdims=True) for hh in heads]
        bcum_c = [jnp.sum(jnp.where(causal, lf_r[hh], 0.0), axis=1, keepdims=True) for hh in heads]
        bcum_r = [jnp.sum(jnp.where(upper, lf_c[hh], 0.0), axis=0, keepdims=True) for hh in heads]
        m_prev = [m_ref[hh] for hh in heads]
        dlog = [jnp.where(causal, bcum_c[hh] - bcum_r[hh] + li_r[hh], NEG_BIG) for hh in heads]
        inter = [bcum_c[hh] + m_prev[hh] for hh in heads]
        m_t = [jnp.maximum(inter[hh], jnp.max(dlog[hh], axis=1, keepdims=True)) for hh in heads]
        w_inter = [jnp.exp(inter[hh] - m_t[hh]) for hh in heads]
        sc = [qk[hh] * jnp.exp(dlog[hh] - m_t[hh]) for hh in heads]
        num = [_dot(sc[hh], v[hh]) + w_inter[hh] * q_c[hh] for hh in heads]
        den = [jnp.sum(sc[hh], axis=1, keepdims=True)
               + w_inter[hh] * jnp.sum(qf[hh] * n_ref[hh], axis=1, keepdims=True)
               for hh in heads]
        inv = [1.0 / jnp.maximum(jnp.abs(den[hh]), jnp.exp(-m_t[hh])) for hh in heads]
        hval = [num[hh] * inv[hh] for hh in heads]
        for hh in heads:
            b_last = bcum_c[hh][CHUNK - 1:CHUNK, :]
            gs_c = b_last - bcum_c[hh] + li_c[hh]
            gs_r = b_last - bcum_r[hh] + li_r[hh]
            m_new = jnp.maximum(b_last + m_prev[hh], jnp.max(gs_r, axis=1, keepdims=True))
            dec = jnp.exp(b_last + m_prev[hh] - m_new)
            kw = k[hh] * jnp.exp(gs_c - m_new)
            c_ref[hh] = dec * c_ref[hh] + _dot_tn(kw, v[hh])
            n_ref[hh] = dec * n_ref[hh] + jnp.sum(kw, axis=0, keepdims=True)
            m_ref[hh] = m_new
        for hh in heads:
            y = _rms_rows(hval[hh], nw_ref[:, vl[hh]])
            y = y * _sigmoid(o_ref[rows, vl[hh]]) * _silu(z_ref[rows, vl[hh]])
            out_ref[rows, vl[hh]] = y.astype(out_ref.dtype)
        return carry

    lax.fori_loop(0, ts // CHUNK, chunk, 0)


def _mlstm(proj, gact, norm_w, batch, seq):
    ts = min(SEQ_BLOCK, seq)
    ns = seq // ts
    kw_, vw = ML_HB * ML_DK, ML_HB * ML_DV
    row = lambda b, h, s: b * ns + s
    qk_blk = lambda off: pl.BlockSpec((ts, kw_), lambda b, h, s: (row(b, h, s), off // kw_ + h))
    v_blk = lambda off: pl.BlockSpec((ts, vw), lambda b, h, s: (row(b, h, s), off // vw + h))
    return pl.pallas_call(
        functools.partial(_mlstm_kernel, ts=ts),
        grid=(batch, ML_HEADS // ML_HB, ns),
        in_specs=[qk_blk(P_ML_Q), qk_blk(P_ML_K), v_blk(P_ML_V), v_blk(P_ML_O), v_blk(P_ML_Z),
                  pl.BlockSpec((ts, GATE_COLS), lambda b, h, s: (row(b, h, s), 0)),
                  pl.BlockSpec((1, vw), lambda b, h, s: (0, h))],
        out_specs=pl.BlockSpec((ts, vw), lambda b, h, s: (row(b, h, s), h)),
        out_shape=jax.ShapeDtypeStruct((batch * seq, ML_WIDTH), BF16),
        scratch_shapes=[pltpu.VMEM((ML_HB, ML_DK, ML_DV), F32), pltpu.VMEM((ML_HB, 1, ML_DK), F32),
                        pltpu.VMEM((ML_HB, 1, 1), F32)],
        compiler_params=pltpu.CompilerParams(
            dimension_semantics=("parallel", "parallel", "arbitrary"),
            vmem_limit_bytes=VMEM_LIMIT),
        name="mlstm_mixer",
    )(proj, proj, proj, proj, proj, gact, norm_w.reshape(1, ML_WIDTH))


def _conv_block(dst_ref, cur_ref, halo_ref, w_ref, first, ts, piece, post):
    back = CONV_K - 1
    for lane0 in range(0, cur_ref.shape[1], piece):
        lanes = slice(lane0, lane0 + piece)
        w = [w_ref[tap:tap + 1, lanes] for tap in range(CONV_K)]
        for r0 in range(0, ts, CHUNK):
            if r0 == 0:
                win = jnp.concatenate([jnp.where(first, 0.0, halo_ref[:, lanes]), cur_ref[0:CHUNK, lanes]],
                                      axis=0)
                taps = [win[HALO - back + tap:HALO - back + tap + CHUNK, :] for tap in range(CONV_K)]
            else:
                taps = [cur_ref[pl.ds(r0 - back + tap, CHUNK), lanes] for tap in range(CONV_K)]
            acc = w[0] * taps[0]
            for tap in range(1, CONV_K):
                acc = acc + w[tap] * taps[tap]
            dst_ref[r0:r0 + CHUNK, lanes] = post(acc, lanes)


GDN_HB = 16


def _gdn_kernel(q_ref, k_ref, v_ref, qh_ref, kh_ref, vh_ref, z_ref, g_ref,
                wq_ref, wk_ref, wv_ref, nw_ref, out_ref,
                s_ref, qs_ref, ks_ref, vs_ref, *, ts):
    head0 = pl.program_id(1) * GDN_HB
    first = pl.program_id(2) == 0
    d = GDN_HEAD_DIM

    @pl.when(first)
    def _():
        s_ref[...] = jnp.zeros_like(s_ref)

    def l2n(u):
        return u * lax.rsqrt(jnp.sum(u * u, axis=-1, keepdims=True) + RMS_EPS)

    _conv_block(ks_ref, k_ref, kh_ref, wk_ref, first, ts, d, lambda u, _: l2n(_silu(u)))
    _conv_block(qs_ref, q_ref, qh_ref, wq_ref, first, ts, d, lambda u, _: l2n(_silu(u)) * (d ** -0.5))
    _conv_block(vs_ref, v_ref, vh_ref, wv_ref, first, ts, d, lambda u, _: _silu(u))

    causal, strict, eye, upper = _chunk_masks()
    eye_f = eye.astype(F32)
    nw = nw_ref[...]

    def chunk(c, carry):
        r0 = pl.multiple_of(c * CHUNK, CHUNK)
        rows = pl.ds(r0, CHUNK)
        g = g_ref[rows, :]
        heads = range(GDN_HB)
        lanes = [slice(hh * d, (hh + 1) * d) for hh in heads]
        k = [ks_ref[rows, lanes[hh]] for hh in heads]
        q = [qs_ref[rows, lanes[hh]] for hh in heads]
        kq = [jnp.concatenate([k[hh], q[hh]], axis=0).astype(BF16) for hh in heads]
        kk_qk = [_dot_nt(kq[hh], kq[hh][:CHUNK]) for hh in heads]
        kq_s = [_dot(kq[hh], s_ref[hh]) for hh in heads]
        b_c = [_gate_column(g, G_GDN_BETA + head0 + hh) for hh in heads]
        g_c = [_gate_column(g, G_GDN_A + head0 + hh) for hh in heads]
        g_r = [jnp.sum(jnp.where(eye, g_c[hh], 0.0), axis=0, keepdims=True) for hh in heads]
        gc_c = [jnp.sum(jnp.where(causal, g_r[hh], 0.0), axis=1, keepdims=True) for hh in heads]
        gc_r = [jnp.sum(jnp.where(upper, g_c[hh], 0.0), axis=0, keepdims=True) for hh in heads]
        decay = [jnp.exp(jnp.where(causal, gc_c[hh] - gc_r[hh], NEG_BIG)) for hh in heads]
        a = [jnp.where(strict, b_c[hh] * kk_qk[hh][:CHUNK] * decay[hh], 0.0) for hh in heads]
        tinv = [eye_f - a[hh] for hh in heads]
        apow = [_dot(a[hh], a[hh]) for hh in heads]
        for step in range(4):
            both = [_dot(jnp.concatenate([apow[hh], tinv[hh]], axis=0), apow[hh]) for hh in heads]
            apow = [both[hh][:CHUNK] for hh in heads]
            tinv = [tinv[hh] + both[hh][CHUNK:] for hh in heads]
        tinv = [tinv[hh] + _dot(tinv[hh], apow[hh]) for hh in heads]
        v = [vs_ref[rows, lanes[hh]] for hh in heads]
        eg_c = [jnp.exp(gc_c[hh]) for hh in heads]
        v_new = [_dot(tinv[hh], b_c[hh] * (v[hh] - eg_c[hh] * kq_s[hh][:CHUNK])) for hh in heads]
        o = [eg_c[hh] * kq_s[hh][CHUNK:] + _dot(kk_qk[hh][CHUNK:] * decay[hh], v_new[hh]) for hh in heads]
        for hh in heads:
            g_last = gc_c[hh][CHUNK - 1:CHUNK, :]
            s_ref[hh] = (s_ref[hh] * jnp.exp(g_last)
                         + _dot_tn(k[hh] * jnp.exp(g_last - gc_c[hh]), v_new[hh]))
        for hh in heads:
            y = _rms_rows(o[hh], nw) * _silu(z_ref[rows, lanes[hh]])
            out_ref[rows, lanes[hh]] = y.astype(out_ref.dtype)
        return carry

    lax.fori_loop(0, ts // CHUNK, chunk, 0)


def _halo_row(b, s, ns, ts):
    return jnp.maximum((b * ns + s) * (ts // HALO) - 1, 0)


def _gdn(proj, gact, conv_w, norm_w, batch, seq):
    ts = min(SEQ_BLOCK, seq)
    ns = seq // ts
    d = GDN_HEAD_DIM
    w = GDN_HB * d
    row = lambda b, h, s: b * ns + s
    cur = lambda off: pl.BlockSpec((ts, w), lambda b, h, s: (row(b, h, s), off // w + h))
    halo = lambda off: pl.BlockSpec((HALO, w), lambda b, h, s: (_halo_row(b, s, ns, ts), off // w + h))
    cw = lambda off: pl.BlockSpec((CONV_K, w), lambda b, h, s: (0, off // w + h))
    return pl.pallas_call(
        functools.partial(_gdn_kernel, ts=ts),
        grid=(batch, GDN_HEADS // GDN_HB, ns),
        in_specs=[cur(P_GDN_Q), cur(P_GDN_K), cur(P_GDN_V),
                  halo(P_GDN_Q), halo(P_GDN_K), halo(P_GDN_V),
                  cur(P_GDN_Z),
                  pl.BlockSpec((ts, GATE_COLS), lambda b, h, s: (row(b, h, s), 0)),
                  cw(0), cw(GDN_WIDTH), cw(2 * GDN_WIDTH),
                  pl.BlockSpec((1, d), lambda b, h, s: (0, 0))],
        out_specs=pl.BlockSpec((ts, w), lambda b, h, s: (row(b, h, s), h)),
        out_shape=jax.ShapeDtypeStruct((batch * seq, GDN_WIDTH), BF16),
        scratch_shapes=[pltpu.VMEM((GDN_HB, d, d), F32)] + [pltpu.VMEM((ts, w), F32)] * 3,
        compiler_params=pltpu.CompilerParams(
            dimension_semantics=("parallel", "parallel", "arbitrary"),
            vmem_limit_bytes=VMEM_LIMIT),
        name="gdn_mixer",
    )(proj, proj, proj, proj, proj, proj, proj, gact, conv_w, conv_w, conv_w, norm_w.reshape(1, d))


SSD_GB = 4
SSD_SEQ_BLOCK = SEQ_BLOCK


def _ssd_kernel(z_ref, x_ref, b_ref, c_ref, xh_ref, bh_ref, ch_ref, g_ref,
                wx_ref, wb_ref, wc_ref, bx_ref, bb_ref, bc_ref, alog_ref, dskip_ref, nw_ref, out_ref,
                st_ref, bufx_ref, bufb_ref, bufc_ref, sel_ref, *, ts):
    group0 = pl.program_id(1) * SSD_GB
    first = pl.program_id(2) == 0
    gw = SSM_GROUP_WIDTH
    hd = SSM_HEAD_DIM
    n = SSM_STATE
    pairs = gw // (2 * hd)

    @pl.when(first)
    def _():
        st_ref[...] = jnp.zeros_like(st_ref)

    _conv_block(bufb_ref, b_ref, bh_ref, wb_ref, first, ts, n, lambda u, l: _silu(u + bb_ref[:, l]))
    _conv_block(bufc_ref, c_ref, ch_ref, wc_ref, first, ts, n, lambda u, l: _silu(u + bc_ref[:, l]))
    _conv_block(bufx_ref, x_ref, xh_ref, wx_ref, first, ts, 2 * n, lambda u, l: _silu(u + bx_ref[:, l]))

    sel_r = lax.broadcasted_iota(jnp.int32, (3 * GATE_COLS, gw), 0) % GATE_COLS
    sel_c = lax.broadcasted_iota(jnp.int32, (3 * GATE_COLS, gw), 1)
    for gg in range(SSD_GB):
        sel_ref[gg] = (sel_r == G_SSM_DT + (group0 + gg) * SSM_GROUP_HEADS + sel_c // hd).astype(BF16)

    ri = lax.broadcasted_iota(jnp.int32, (CHUNK, gw), 0)
    cm = lax.broadcasted_iota(jnp.int32, (CHUNK, gw), 1) % hd
    causal_w = cm <= ri
    upper_w = ri <= cm
    tri_b = _chunk_masks()[0].astype(BF16)
    lane_lo = lax.broadcasted_iota(jnp.int32, (CHUNK, 2 * hd), 1) < hd

    def chunk(c, carry):
        r0 = pl.multiple_of(c * CHUNK, CHUNK)
        rows = pl.ds(r0, CHUNK)
        g = g_ref[rows, :]
        groups = range(SSD_GB)
        xl = [slice(gg * gw, (gg + 1) * gw) for gg in groups]
        nl = [slice(gg * n, (gg + 1) * n) for gg in groups]
        g3 = _split3(g)
        cum3 = jnp.dot(tri_b, g3, preferred_element_type=F32)
        gcum = cum3[:, :GATE_COLS] + cum3[:, GATE_COLS:2 * GATE_COLS] + cum3[:, 2 * GATE_COLS:]
        both3 = jnp.concatenate([g3, _split3(gcum)], axis=0)
        expanded = [jnp.dot(both3, sel_ref[gg], preferred_element_type=F32) for gg in groups]
        neg_a = [-jnp.exp(alog_ref[:, xl[gg]]) for gg in groups]
        dt = [expanded[gg][:CHUNK] for gg in groups]
        a = [dt[gg] * neg_a[gg] for gg in groups]
        acs = [expanded[gg][CHUNK:] * neg_a[gg] for gg in groups]
        bm = [bufb_ref[rows, nl[gg]] for gg in groups]
        cmat = [bufc_ref[rows, nl[gg]] for gg in groups]
        cb2 = [_dot_nt(cmat[gg], jnp.concatenate([bm[gg], bm[gg]], axis=0)) for gg in groups]
        y_off = [_dot(cmat[gg], st_ref[gg]) for gg in groups]
        x = [bufx_ref[rows, xl[gg]] for gg in groups]
        xdt = [x[gg] * dt[gg] for gg in groups]
        acs_r = [jnp.sum(jnp.where(upper_w, a[gg], 0.0), axis=0, keepdims=True) for gg in groups]
        lm = [jnp.exp(jnp.where(causal_w, acs[gg] - acs_r[gg], NEG_BIG)) for gg in groups]
        mm = [(jnp.concatenate([cb2[gg]] * pairs, axis=1) * lm[gg]).astype(BF16) for gg in groups]
        y_diag = []
        for gg in groups:
            parts = []
            for p in range(pairs):
                xp = xdt[gg][:, p * 2 * hd:(p + 1) * 2 * hd]
                rhs = jnp.concatenate([jnp.where(lane_lo, xp, 0.0), jnp.where(lane_lo, 0.0, xp)], axis=0)
                parts.append(_dot(mm[gg][:, p * 2 * hd:(p + 1) * 2 * hd], rhs))
            y_diag.append(jnp.concatenate(parts, axis=1))
        for gg in groups:
            a_last = acs[gg][CHUNK - 1:CHUNK, :]
            st_ref[gg] = (st_ref[gg] * jnp.exp(a_last)
                          + _dot_tn(bm[gg], xdt[gg] * jnp.exp(a_last - acs[gg])))
        for gg in groups:
            y = y_diag[gg] + y_off[gg] * jnp.exp(acs[gg]) + dskip_ref[:, xl[gg]] * x[gg]
            y = y * _silu(z_ref[rows, xl[gg]])
            out_ref[rows, xl[gg]] = _rms_rows(y, nw_ref[:, xl[gg]]).astype(out_ref.dtype)
        return carry

    lax.fori_loop(0, ts // CHUNK, chunk, 0)


def _ssd(proj, gact, conv_w, conv_b, a_log, dskip, norm_w, batch, seq):
    ts = min(SSD_SEQ_BLOCK, seq)
    ns = seq // ts
    gw, n = SSD_GB * SSM_GROUP_WIDTH, SSD_GB * SSM_STATE
    row = lambda b, g, s: b * ns + s
    cur = lambda off, w: pl.BlockSpec((ts, w), lambda b, g, s: (row(b, g, s), off // w + g))
    halo = lambda off, w: pl.BlockSpec((HALO, w), lambda b, g, s: (_halo_row(b, s, ns, ts), off // w + g))
    par = lambda rows_, off, w: pl.BlockSpec((rows_, w), lambda b, g, s: (0, off // w + g))
    expand = lambda p: jnp.repeat(p, SSM_HEAD_DIM).reshape(1, SSM_WIDTH)
    conv_b = conv_b.reshape(1, -1)
    return pl.pallas_call(
        functools.partial(_ssd_kernel, ts=ts),
        grid=(batch, SSM_GROUPS // SSD_GB, ns),
        in_specs=[cur(P_SSM_Z, gw), cur(P_SSM_X, gw), cur(P_SSM_B, n), cur(P_SSM_C, n),
                  halo(P_SSM_X, gw), halo(P_SSM_B, n), halo(P_SSM_C, n),
                  pl.BlockSpec((ts, GATE_COLS), lambda b, g, s: (row(b, g, s), 0)),
                  par(CONV_K, 0, gw), par(CONV_K, P_SSM_B - P_SSM_X, n), par(CONV_K, P_SSM_C - P_SSM_X, n),
                  par(1, 0, gw), par(1, P_SSM_B - P_SSM_X, n), par(1, P_SSM_C - P_SSM_X, n),
                  par(1, 0, gw), par(1, 0, gw), par(1, 0, gw)],
        out_specs=pl.BlockSpec((ts, gw), lambda b, g, s: (row(b, g, s), g)),
        out_shape=jax.ShapeDtypeStruct((batch * seq, SSM_WIDTH), BF16),
        scratch_shapes=[pltpu.VMEM((SSD_GB, SSM_STATE, SSM_GROUP_WIDTH), F32),
                        pltpu.VMEM((ts, gw), F32), pltpu.VMEM((ts, n), F32), pltpu.VMEM((ts, n), F32),
                        pltpu.VMEM((SSD_GB, 3 * GATE_COLS, SSM_GROUP_WIDTH), BF16)],
        compiler_params=pltpu.CompilerParams(
            dimension_semantics=("parallel", "parallel", "arbitrary"),
            vmem_limit_bytes=VMEM_LIMIT),
        name="ssd_mixer",
    )(proj, proj, proj, proj, proj, proj, proj, gact, conv_w, conv_w, conv_w, conv_b, conv_b, conv_b,
      expand(a_log), expand(dskip), norm_w.reshape(1, SSM_WIDTH))


def _deepnorm_kernel(y_ref, x_ref, g_ref, b_ref, o_ref, obf_ref):
    u = DEEPNORM_ALPHA * x_ref[...] + y_ref[...]
    mu = jnp.mean(u, axis=-1, keepdims=True)
    var = jnp.mean(jnp.square(u - mu), axis=-1, keepdims=True)
    out = (u - mu) * lax.rsqrt(var + LN_EPS) * g_ref[...] + b_ref[...]
    o_ref[...] = out
    obf_ref[...] = out.astype(BF16)


def _deepnorm(y, x, ln_g, ln_b, tm=256):
    m, d = x.shape
    tm = min(tm, m)
    blk = pl.BlockSpec((tm, d), lambda i: (i, 0))
    vec = pl.BlockSpec((1, d), lambda i: (0, 0))
    return pl.pallas_call(
        _deepnorm_kernel,
        grid=(m // tm,),
        in_specs=[blk, blk, vec, vec],
        out_specs=[blk, blk],
        out_shape=[jax.ShapeDtypeStruct((m, d), F32), jax.ShapeDtypeStruct((m, d), BF16)],
        compiler_params=pltpu.CompilerParams(dimension_semantics=("parallel",),
                                             vmem_limit_bytes=VMEM_LIMIT),
        name="deepnorm_ln",
    )(y, x, ln_g.reshape(1, d), ln_b.reshape(1, d))


def _gate_rows(ml_i_bias, ml_f_bias, gdn_dt_bias, gdn_a_log, ssm_dt_bias):
    zeros = lambda n: jnp.zeros((n,), F32)
    bias = jnp.concatenate([ml_i_bias, ml_f_bias, zeros(GDN_HEADS), gdn_dt_bias, ssm_dt_bias,
                            zeros(GATE_COLS - G_END)])
    alog = jnp.concatenate([zeros(G_GDN_A), gdn_a_log, zeros(GATE_COLS - G_SSM_DT)])
    return bias.reshape(1, GATE_COLS), alog.reshape(1, GATE_COLS)


def kernel(x, w_in, w_out, ml_i_bias, ml_f_bias, ml_norm_w, gdn_conv_w, gdn_A_log, gdn_dt_bias,
           gdn_norm_w, ssm_conv_w, ssm_conv_b, ssm_A_log, ssm_dt_bias, ssm_D, ssm_norm_w, ln_g, ln_b):
    batch, seq, d = x.shape
    xf = x.reshape(batch * seq, d).astype(F32)
    xb = None
    w_t = jnp.swapaxes(w_in.astype(F32), 1, 2)
    w_o = _cast_bf16(w_out.astype(F32))
    for l in range(w_in.shape[0]):
        bias_row, alog_row = _gate_rows(ml_i_bias[l], ml_f_bias[l], gdn_dt_bias[l], gdn_A_log[l],
                                        ssm_dt_bias[l])
        if xb is None:
            gact, xb = _gates(xf, w_t, l, bias_row, alog_row, True)
        else:
            gact, = _gates(xf, w_t, l, bias_row, alog_row, False)
        y_ml = _mlstm(_proj(xb, w_t, l, 0, ML_MAIN), gact, ml_norm_w[l], batch, seq)
        y_gdn = _gdn(_proj(xb, w_t, l, ML_COLS, GDN_MAIN), gact, gdn_conv_w[l], gdn_norm_w[l],
                     batch, seq)
        y_ssm = _ssd(_proj(xb, w_t, l, ML_COLS + GDN_COLS, SSM_MAIN), gact, ssm_conv_w[l],
                     ssm_conv_b[l], ssm_A_log[l], ssm_D[l], ssm_norm_w[l], batch, seq)
        y = _out_matmul(y_ml, y_gdn, y_ssm, w_o, l)
        xf, xb = _deepnorm(y, xf, ln_g[l], ln_b[l])
    return xf.reshape(batch, seq, d).astype(x.dtype)
```

```python
import functools

import jax
import jax.numpy as jnp
from jax import lax
from jax.experimental import pallas as pl
from jax.experimental.pallas import tpu as pltpu

F32 = jnp.float32
BF16 = jnp.bfloat16

D_MODEL = 4096
DEPTH = 2
CHUNK = 64
CONV_K = 4
ML_WIDTH = D_MODEL // 2
ML_HEADS = 4
ML_DV = ML_WIDTH // ML_HEADS
ML_DK = ML_DV // 2
GDN_WIDTH = D_MODEL // 2
GDN_HEAD_DIM = 128
GDN_HEADS = GDN_WIDTH // GDN_HEAD_DIM
SSM_WIDTH = D_MODEL
SSM_HEAD_DIM = 64
SSM_HEADS = SSM_WIDTH // SSM_HEAD_DIM
SSM_GROUPS = 8
SSM_STATE = 128
SSM_GROUP_WIDTH = SSM_WIDTH // SSM_GROUPS
SSM_GROUP_HEADS = SSM_HEADS // SSM_GROUPS
MIX_WIDTH = ML_WIDTH + GDN_WIDTH + SSM_WIDTH
DEEPNORM_ALPHA = (2 * DEPTH) ** 0.25
RMS_EPS = 1e-6
LN_EPS = 1e-5
NEG_BIG = -1e30

ML_MAIN = ML_HEADS * ML_DK * 2 + 3 * ML_WIDTH
ML_COLS = ML_MAIN + 2 * ML_HEADS
GDN_MAIN = 4 * GDN_WIDTH
GDN_COLS = GDN_MAIN + 2 * GDN_HEADS
SSM_MAIN = 2 * SSM_WIDTH + 2 * SSM_GROUPS * SSM_STATE
SSM_COLS = SSM_MAIN + SSM_HEADS

P_ML_Q = 0
P_ML_K = P_ML_Q + ML_HEADS * ML_DK
P_ML_V = P_ML_K + ML_HEADS * ML_DK
P_ML_O = P_ML_V + ML_WIDTH
P_ML_Z = P_ML_O + ML_WIDTH
P_GDN_Q = 0
P_GDN_K = P_GDN_Q + GDN_WIDTH
P_GDN_V = P_GDN_K + GDN_WIDTH
P_GDN_Z = P_GDN_V + GDN_WIDTH
P_SSM_Z = 0
P_SSM_X = P_SSM_Z + SSM_WIDTH
P_SSM_B = P_SSM_X + SSM_WIDTH
P_SSM_C = P_SSM_B + SSM_GROUPS * SSM_STATE

GATE_COLS = 128
G_ML_I = 0
G_ML_F = G_ML_I + ML_HEADS
G_GDN_BETA = G_ML_F + ML_HEADS
G_GDN_A = G_GDN_BETA + GDN_HEADS
G_SSM_DT = G_GDN_A + GDN_HEADS
G_END = G_SSM_DT + SSM_HEADS

HALO = 8
SEQ_BLOCK = 512
VMEM_LIMIT = 56 * 1024 * 1024


def _sigmoid(u):
    return 0.5 * jnp.tanh(0.5 * u) + 0.5


def _silu(u):
    h = 0.5 * u
    return h * jnp.tanh(h) + h


def _softplus(u):
    return jnp.maximum(u, 0.0) + jnp.log1p(jnp.exp(-jnp.abs(u)))


def _dot(a, b):
    return jnp.dot(a.astype(BF16), b.astype(BF16), preferred_element_type=F32)


def _dot_nt(a, b):
    return lax.dot_general(a.astype(BF16), b.astype(BF16), (((1,), (1,)), ((), ())),
                           preferred_element_type=F32)


def _dot_tn(a, b):
    return lax.dot_general(a.astype(BF16), b.astype(BF16), (((0,), (0,)), ((), ())),
                           preferred_element_type=F32)


def _split3(u):
    hi = u.astype(BF16)
    rest = u - hi.astype(F32)
    mid = rest.astype(BF16)
    lo = (rest - mid.astype(F32)).astype(BF16)
    return jnp.concatenate([hi, mid, lo], axis=1)


def _rms_rows(u, w_row):
    return u * lax.rsqrt(jnp.mean(u * u, axis=-1, keepdims=True) + RMS_EPS) * w_row


def _chunk_masks():
    ri = lax.broadcasted_iota(jnp.int32, (CHUNK, CHUNK), 0)
    ci = lax.broadcasted_iota(jnp.int32, (CHUNK, CHUNK), 1)
    return ci <= ri, ci < ri, ci == ri, ri <= ci


def _gate_column(g, col):
    lane = lax.broadcasted_iota(jnp.int32, g.shape, 1)
    return jnp.sum(jnp.where(lane == col, g, 0.0), axis=1, keepdims=True)


LANES = 128
PACK_ROWS = 1024
PACK_COLS = 512
PACK_SUB = 256


def _cast_kernel(w_ref, o_ref):
    def body(r, carry):
        rows = pl.ds(pl.multiple_of(r * PACK_SUB, PACK_SUB), PACK_SUB)
        o_ref[rows, :] = w_ref[rows, :].astype(BF16)
        return carry

    lax.fori_loop(0, PACK_ROWS // PACK_SUB, body, 0)


def _cast_bf16(w):
    layers, rows, cols = w.shape
    assert rows % PACK_ROWS == 0 and cols % PACK_COLS == 0
    blk = pl.BlockSpec((None, PACK_ROWS, PACK_COLS), lambda l, r, j: (l, r, j))
    return pl.pallas_call(
        _cast_kernel,
        grid=(layers, rows // PACK_ROWS, cols // PACK_COLS),
        in_specs=[blk],
        out_specs=blk,
        out_shape=jax.ShapeDtypeStruct((layers, rows, cols), BF16),
        compiler_params=pltpu.CompilerParams(
            dimension_semantics=("parallel", "parallel", "parallel"), vmem_limit_bytes=VMEM_LIMIT),
        name="cast_weights",
    )(w)


PROJ_TM = 512
PROJ_TN = 1024


def _proj_kernel(x_ref, wt_ref, o_ref, wbf_ref):
    @pl.when(pl.program_id(1) == 0)
    def _():
        def body(r, carry):
            rows = pl.ds(pl.multiple_of(r * LANES, LANES), LANES)
            wbf_ref[rows, :] = wt_ref[rows, :].astype(BF16)
            return carry

        lax.fori_loop(0, wt_ref.shape[0] // LANES, body, 0)

    o_ref[...] = lax.dot_general(x_ref[...], wbf_ref[...], (((1,), (1,)), ((), ())),
                                 preferred_element_type=F32)


def _proj(x, w_t, layer, start, width):
    m, kdim = x.shape
    tm, tn = min(PROJ_TM, m), PROJ_TN
    assert start % 8 == 0 and width % tn == 0 and m % tm == 0
    return pl.pallas_call(
        _proj_kernel,
        grid=(width // tn, m // tm),
        in_specs=[pl.BlockSpec((tm, kdim), lambda j, i: (i, 0)),
                  pl.BlockSpec((None, pl.Element(tn), pl.Element(kdim)),
                               lambda j, i: (layer, (start // 8 + j * (tn // 8)) * 8, 0))],
        out_specs=pl.BlockSpec((tm, tn), lambda j, i: (i, j)),
        out_shape=jax.ShapeDtypeStruct((m, width), F32),
        scratch_shapes=[pltpu.VMEM((tn, kdim), BF16)],
        compiler_params=pltpu.CompilerParams(
            dimension_semantics=("parallel", "arbitrary"), vmem_limit_bytes=VMEM_LIMIT),
        name="proj_matmul",
    )(x, w_t)


def _out_mm_kernel(yml_ref, ygdn_ref, yssm_ref, w_ref, o_ref, acc_ref, *, nk):
    k = pl.program_id(2)

    @pl.when(k == 0)
    def _():
        acc_ref[...] = jnp.dot(yml_ref[...], w_ref[...], preferred_element_type=F32)

    @pl.when(k == 1)
    def _():
        acc_ref[...] += jnp.dot(ygdn_ref[...], w_ref[...], preferred_element_type=F32)

    @pl.when(jnp.logical_and(k >= 2, k < nk - 1))
    def _():
        acc_ref[...] += jnp.dot(yssm_ref[...], w_ref[...], preferred_element_type=F32)

    @pl.when(k == nk - 1)
    def _():
        o_ref[...] = acc_ref[...] + jnp.dot(yssm_ref[...], w_ref[...], preferred_element_type=F32)


def _out_matmul(y_ml, y_gdn, y_ssm, w, layer, tm=1024, tn=1024):
    m = y_ml.shape[0]
    n = w.shape[2]
    tk = ML_WIDTH
    assert y_ml.shape[1] == tk and y_gdn.shape[1] == tk and y_ssm.shape[1] % tk == 0
    nk = w.shape[1] // tk
    tm, tn = min(tm, m), min(tn, n)
    return pl.pallas_call(
        functools.partial(_out_mm_kernel, nk=nk),
        grid=(n // tn, m // tm, nk),
        in_specs=[pl.BlockSpec((tm, tk), lambda j, i, k: (i, 0)),
                  pl.BlockSpec((tm, tk), lambda j, i, k: (i, 0)),
                  pl.BlockSpec((tm, tk), lambda j, i, k: (i, jnp.maximum(k - 2, 0))),
                  pl.BlockSpec((None, tk, tn), lambda j, i, k: (layer, k, j))],
        out_specs=pl.BlockSpec((tm, tn), lambda j, i, k: (i, j)),
        out_shape=jax.ShapeDtypeStruct((m, n), F32),
        scratch_shapes=[pltpu.VMEM((tm, tn), F32)],
        compiler_params=pltpu.CompilerParams(
            dimension_semantics=("parallel", "parallel", "arbitrary"),
            vmem_limit_bytes=VMEM_LIMIT),
        name="out_matmul",
    )(y_ml, y_gdn, y_ssm, w)


def _gate_block(x, wml_ref, wgdn_ref, wssm_ref, bias_ref, alog_ref):
    pad = jnp.zeros((GATE_COLS - G_END, x.shape[1]), F32)
    w_gate = jnp.concatenate([wml_ref[...], wgdn_ref[...], wssm_ref[...], pad], axis=0)
    x_hi = x.astype(BF16)
    x_lo = (x - x_hi.astype(F32)).astype(BF16)
    w_hi = w_gate.astype(BF16)
    w_lo = (w_gate - w_hi.astype(F32)).astype(BF16)
    raw = _dot_nt(x_hi, w_hi) + (_dot_nt(x_hi, w_lo) + _dot_nt(x_lo, w_hi))
    col = lax.broadcasted_iota(jnp.int32, raw.shape, 1)
    u = raw + bias_ref[...]
    sp = _softplus(u)
    out = jnp.where(col < G_ML_F, u,
          jnp.where(col < G_GDN_BETA, u - sp,
          jnp.where(col < G_GDN_A, _sigmoid(u),
          jnp.where(col < G_SSM_DT, -jnp.exp(alog_ref[...]) * sp,
                    sp))))
    return out, x_hi


def _gate_weight_specs(kdim, layer, index_args):
    gate_rows = ((ML_MAIN, 2 * ML_HEADS), (ML_COLS + GDN_MAIN, 2 * GDN_HEADS),
                 (ML_COLS + GDN_COLS + SSM_MAIN, SSM_HEADS))
    assert all(s % 8 == 0 and n % 8 == 0 for s, n in gate_rows)
    return [pl.BlockSpec((None, pl.Element(n), pl.Element(kdim)), lambda *_, s=s: (layer, s, 0))
            for s, n in gate_rows]


def _gates_kernel(x_ref, wml_ref, wgdn_ref, wssm_ref, bias_ref, alog_ref, o_ref, *maybe_xb_ref):
    out, x_hi = _gate_block(x_ref[...], wml_ref, wgdn_ref, wssm_ref, bias_ref, alog_ref)
    for xb_ref in maybe_xb_ref:
        xb_ref[...] = x_hi
    o_ref[...] = out


def _gates(x, w_t, layer, bias_row, alog_row, emit_bf16, tm=512):
    m, kdim = x.shape
    tm = min(tm, m)
    row = pl.BlockSpec((1, GATE_COLS), lambda i: (0, 0))
    out_specs = [pl.BlockSpec((tm, GATE_COLS), lambda i: (i, 0))]
    out_shape = [jax.ShapeDtypeStruct((m, GATE_COLS), F32)]
    if emit_bf16:
        out_specs.append(pl.BlockSpec((tm, kdim), lambda i: (i, 0)))
        out_shape.append(jax.ShapeDtypeStruct((m, kdim), BF16))
    return pl.pallas_call(
        _gates_kernel,
        grid=(m // tm,),
        in_specs=[pl.BlockSpec((tm, kdim), lambda i: (i, 0))] + _gate_weight_specs(kdim, layer, None) + [row, row],
        out_specs=out_specs,
        out_shape=out_shape,
        compiler_params=pltpu.CompilerParams(dimension_semantics=("parallel",),
                                             vmem_limit_bytes=VMEM_LIMIT),
        name="gate_proj",
    )(x, w_t, w_t, w_t, bias_row, alog_row)


ML_HB = 4


def _mlstm_kernel(q_ref, k_ref, v_ref, o_ref, z_ref, g_ref, nw_ref, out_ref,
                  c_ref, n_ref, m_ref, *, ts):
    head0 = pl.program_id(1) * ML_HB

    @pl.when(pl.program_id(2) == 0)
    def _():
        c_ref[...] = jnp.zeros_like(c_ref)
        n_ref[...] = jnp.zeros_like(n_ref)
        m_ref[...] = jnp.zeros_like(m_ref)

    causal, _, eye, upper = _chunk_masks()

    def chunk(c, carry):
        r0 = pl.multiple_of(c * CHUNK, CHUNK)
        rows = pl.ds(r0, CHUNK)
        g = g_ref[rows, :]
        heads = range(ML_HB)
        kl = [slice(hh * ML_DK, (hh + 1) * ML_DK) for hh in heads]
        vl = [slice(hh * ML_DV, (hh + 1) * ML_DV) for hh in heads]
        qf = [q_ref[rows, kl[hh]] * (ML_DK ** -0.5) for hh in heads]
        q = [qf[hh].astype(BF16) for hh in heads]
        k = [k_ref[rows, kl[hh]] for hh in heads]
        v = [v_ref[rows, vl[hh]].astype(BF16) for hh in heads]
        qk = [_dot_nt(q[hh], k[hh]) for hh in heads]
        q_c = [_dot(q[hh], c_ref[hh]) for hh in heads]
        li_c = [_gate_column(g, G_ML_I + head0 + hh) for hh in heads]
        lf_c = [_gate_column(g, G_ML_F + head0 + hh) for hh in heads]
        li_r = [jnp.sum(jnp.where(eye, li_c[hh], 0.0), axis=0, keepdims=True) for hh in heads]
        lf_r = [jnp.sum(jnp.where(eye, lf_c[hh], 0.0), axis=0, keepdims=True) for hh in heads]
        bcum_c = [jnp.sum(jnp.where(causal, lf_r[hh], 0.0), axis=1, keepdims=True) for hh in heads]
        bcum_r = [jnp.sum(jnp.where(upper, lf_c[hh], 0.0), axis=0, keepdims=True) for hh in heads]
        m_prev = [m_ref[hh] for hh in heads]
        dlog = [jnp.where(causal, bcum_c[hh] - bcum_r[hh] + li_r[hh], NEG_BIG) for hh in heads]
        inter = [bcum_c[hh] + m_prev[hh] for hh in heads]
        m_t = [jnp.maximum(inter[hh], jnp.max(dlog[hh], axis=1, keepdims=True)) for hh in heads]
        w_inter = [jnp.exp(inter[hh] - m_t[hh]) for hh in heads]
        sc = [qk[hh] * jnp.exp(dlog[hh] - m_t[hh]) for hh in heads]
        num = [_dot(sc[hh], v[hh]) + w_inter[hh] * q_c[hh] for hh in heads]
        den = [jnp.sum(sc[hh], axis=1, keepdims=True)
               + w_inter[hh] * jnp.sum(qf[hh] * n_ref[hh], axis=1, keepdims=True)
               for hh in heads]
        inv = [1.0 / jnp.maximum(jnp.abs(den[hh]), jnp.exp(-m_t[hh])) for hh in heads]
        hval = [num[hh] * inv[hh] for hh in heads]
        for hh in heads:
            b_last = bcum_c[hh][CHUNK - 1:CHUNK, :]
            gs_c = b_last - bcum_c[hh] + li_c[hh]
            gs_r = b_last - bcum_r[hh] + li_r[hh]
            m_new = jnp.maximum(b_last + m_prev[hh], jnp.max(gs_r, axis=1, keepdims=True))
            dec = jnp.exp(b_last + m_prev[hh] - m_new)
            kw = k[hh] * jnp.exp(gs_c - m_new)
            c_ref[hh] = dec * c_ref[hh] + _dot_tn(kw, v[hh])
            n_ref[hh] = dec * n_ref[hh] + jnp.sum(kw, axis=0, keepdims=True)
            m_ref[hh] = m_new
        for hh in heads:
            y = _rms_rows(hval[hh], nw_ref[:, vl[hh]])
            y = y * _sigmoid(o_ref[rows, vl[hh]]) * _silu(z_ref[rows, vl[hh]])
            out_ref[rows, vl[hh]] = y.astype(out_ref.dtype)
        return carry

    lax.fori_loop(0, ts // CHUNK, chunk, 0)


def _mlstm(proj, gact, norm_w, batch, seq):
    ts = min(SEQ_BLOCK, seq)
    ns = seq // ts
    kw_, vw = ML_HB * ML_DK, ML_HB * ML_DV
    row = lambda b, h, s: b * ns + s
    qk_blk = lambda off: pl.BlockSpec((ts, kw_), lambda b, h, s: (row(b, h, s), off // kw_ + h))
    v_blk = lambda off: pl.BlockSpec((ts, vw), lambda b, h, s: (row(b, h, s), off // vw + h))
    return pl.pallas_call(
        functools.partial(_mlstm_kernel, ts=ts),
        grid=(batch, ML_HEADS // ML_HB, ns),
        in_specs=[qk_blk(P_ML_Q), qk_blk(P_ML_K), v_blk(P_ML_V), v_blk(P_ML_O), v_blk(P_ML_Z),
                  pl.BlockSpec((ts, GATE_COLS), lambda b, h, s: (row(b, h, s), 0)),
                  pl.BlockSpec((1, vw), lambda b, h, s: (0, h))],
        out_specs=pl.BlockSpec((ts, vw), lambda b, h, s: (row(b, h, s), h)),
        out_shape=jax.ShapeDtypeStruct((batch * seq, ML_WIDTH), BF16),
        scratch_shapes=[pltpu.VMEM((ML_HB, ML_DK, ML_DV), F32), pltpu.VMEM((ML_HB, 1, ML_DK), F32),
                        pltpu.VMEM((ML_HB, 1, 1), F32)],
        compiler_params=pltpu.CompilerParams(
            dimension_semantics=("parallel", "parallel", "arbitrary"),
            vmem_limit_bytes=VMEM_LIMIT),
        name="mlstm_mixer",
    )(proj, proj, proj, proj, proj, gact, norm_w.reshape(1, ML_WIDTH))


def _conv_block(dst_ref, cur_ref, halo_ref, w_ref, first, ts, piece, post):
    back = CONV_K - 1
    for lane0 in range(0, cur_ref.shape[1], piece):
        lanes = slice(lane0, lane0 + piece)
        w = [w_ref[tap:tap + 1, lanes] for tap in range(CONV_K)]
        for r0 in range(0, ts, CHUNK):
            if r0 == 0:
                win = jnp.concatenate([jnp.where(first, 0.0, halo_ref[:, lanes]), cur_ref[0:CHUNK, lanes]],
                                      axis=0)
                taps = [win[HALO - back + tap:HALO - back + tap + CHUNK, :] for tap in range(CONV_K)]
            else:
                taps = [cur_ref[pl.ds(r0 - back + tap, CHUNK), lanes] for tap in range(CONV_K)]
            acc = w[0] * taps[0]
            for tap in range(1, CONV_K):
                acc = acc + w[tap] * taps[tap]
            dst_ref[r0:r0 + CHUNK, lanes] = post(acc, lanes)


GDN_HB = 16


def _gdn_kernel(q_ref, k_ref, v_ref, qh_ref, kh_ref, vh_ref, z_ref, g_ref,
                wq_ref, wk_ref, wv_ref, nw_ref, out_ref,
                s_ref, qs_ref, ks_ref, vs_ref, *, ts):
    head0 = pl.program_id(1) * GDN_HB
    first = pl.program_id(2) == 0
    d = GDN_HEAD_DIM

    @pl.when(first)
    def _():
        s_ref[...] = jnp.zeros_like(s_ref)

    def l2n(u):
        return u * lax.rsqrt(jnp.sum(u * u, axis=-1, keepdims=True) + RMS_EPS)

    _conv_block(ks_ref, k_ref, kh_ref, wk_ref, first, ts, d, lambda u, _: l2n(_silu(u)))
    _conv_block(qs_ref, q_ref, qh_ref, wq_ref, first, ts, d, lambda u, _: l2n(_silu(u)) * (d ** -0.5))
    _conv_block(vs_ref, v_ref, vh_ref, wv_ref, first, ts, d, lambda u, _: _silu(u))

    causal, strict, eye, upper = _chunk_masks()
    eye_f = eye.astype(F32)
    nw = nw_ref[...]

    def chunk(c, carry):
        r0 = pl.multiple_of(c * CHUNK, CHUNK)
        rows = pl.ds(r0, CHUNK)
        g = g_ref[rows, :]
        heads = range(GDN_HB)
        lanes = [slice(hh * d, (hh + 1) * d) for hh in heads]
        k = [ks_ref[rows, lanes[hh]] for hh in heads]
        q = [qs_ref[rows, lanes[hh]] for hh in heads]
        kq = [jnp.concatenate([k[hh], q[hh]], axis=0).astype(BF16) for hh in heads]
        kk_qk = [_dot_nt(kq[hh], kq[hh][:CHUNK]) for hh in heads]
        kq_s = [_dot(kq[hh], s_ref[hh]) for hh in heads]
        b_c = [_gate_column(g, G_GDN_BETA + head0 + hh) for hh in heads]
        g_c = [_gate_column(g, G_GDN_A + head0 + hh) for hh in heads]
        g_r = [jnp.sum(jnp.where(eye, g_c[hh], 0.0), axis=0, keepdims=True) for hh in heads]
        gc_c = [jnp.sum(jnp.where(causal, g_r[hh], 0.0), axis=1, keepdims=True) for hh in heads]
        gc_r = [jnp.sum(jnp.where(upper, g_c[hh], 0.0), axis=0, keepdims=True) for hh in heads]
        decay = [jnp.exp(jnp.where(causal, gc_c[hh] - gc_r[hh], NEG_BIG)) for hh in heads]
        a = [jnp.where(strict, b_c[hh] * kk_qk[hh][:CHUNK] * decay[hh], 0.0) for hh in heads]
        tinv = [eye_f - a[hh] for hh in heads]
        apow = [_dot(a[hh], a[hh]) for hh in heads]
        for step in range(4):
            both = [_dot(jnp.concatenate([apow[hh], tinv[hh]], axis=0), apow[hh]) for hh in heads]
            apow = [both[hh][:CHUNK] for hh in heads]
            tinv = [tinv[hh] + both[hh][CHUNK:] for hh in heads]
        tinv = [tinv[hh] + _dot(tinv[hh], apow[hh]) for hh in heads]
        v = [vs_ref[rows, lanes[hh]] for hh in heads]
        eg_c = [jnp.exp(gc_c[hh]) for hh in heads]
        v_new = [_dot(tinv[hh], b_c[hh] * (v[hh] - eg_c[hh] * kq_s[hh][:CHUNK])) for hh in heads]
        o = [eg_c[hh] * kq_s[hh][CHUNK:] + _dot(kk_qk[hh][CHUNK:] * decay[hh], v_new[hh]) for hh in heads]
        for hh in heads:
            g_last = gc_c[hh][CHUNK - 1:CHUNK, :]
            s_ref[hh] = (s_ref[hh] * jnp.exp(g_last)
                         + _dot_tn(k[hh] * jnp.exp(g_last - gc_c[hh]), v_new[hh]))
        for hh in heads:
            y = _rms_rows(o[hh], nw) * _silu(z_ref[rows, lanes[hh]])
            out_ref[rows, lanes[hh]] = y.astype(out_ref.dtype)
        return carry

    lax.fori_loop(0, ts // CHUNK, chunk, 0)


def _halo_row(b, s, ns, ts):
    return jnp.maximum((b * ns + s) * (ts // HALO) - 1, 0)


def _gdn(proj, gact, conv_w, norm_w, batch, seq):
    ts = min(SEQ_BLOCK, seq)
    ns = seq // ts
    d = GDN_HEAD_DIM
    w = GDN_HB * d
    row = lambda b, h, s: b * ns + s
    cur = lambda off: pl.BlockSpec((ts, w), lambda b, h, s: (row(b, h, s), off // w + h))
    halo = lambda off: pl.BlockSpec((HALO, w), lambda b, h, s: (_halo_row(b, s, ns, ts), off // w + h))
    cw = lambda off: pl.BlockSpec((CONV_K, w), lambda b, h, s: (0, off // w + h))
    return pl.pallas_call(
        functools.partial(_gdn_kernel, ts=ts),
        grid=(batch, GDN_HEADS // GDN_HB, ns),
        in_specs=[cur(P_GDN_Q), cur(P_GDN_K), cur(P_GDN_V),
                  halo(P_GDN_Q), halo(P_GDN_K), halo(P_GDN_V),
                  cur(P_GDN_Z),
                  pl.BlockSpec((ts, GATE_COLS), lambda b, h, s: (row(b, h, s), 0)),
                  cw(0), cw(GDN_WIDTH), cw(2 * GDN_WIDTH),
                  pl.BlockSpec((1, d), lambda b, h, s: (0, 0))],
        out_specs=pl.BlockSpec((ts, w), lambda b, h, s: (row(b, h, s), h)),
        out_shape=jax.ShapeDtypeStruct((batch * seq, GDN_WIDTH), BF16),
        scratch_shapes=[pltpu.VMEM((GDN_HB, d, d), F32)] + [pltpu.VMEM((ts, w), F32)] * 3,
        compiler_params=pltpu.CompilerParams(
            dimension_semantics=("parallel", "parallel", "arbitrary"),
            vmem_limit_bytes=VMEM_LIMIT),
        name="gdn_mixer",
    )(proj, proj, proj, proj, proj, proj, proj, gact, conv_w, conv_w, conv_w, norm_w.reshape(1, d))


SSD_GB = 4
SSD_SEQ_BLOCK = SEQ_BLOCK


def _ssd_kernel(z_ref, x_ref, b_ref, c_ref, xh_ref, bh_ref, ch_ref, g_ref,
                wx_ref, wb_ref, wc_ref, bx_ref, bb_ref, bc_ref, alog_ref, dskip_ref, nw_ref, out_ref,
                st_ref, bufx_ref, bufb_ref, bufc_ref, sel_ref, *, ts):
    group0 = pl.program_id(1) * SSD_GB
    first = pl.program_id(2) == 0
    gw = SSM_GROUP_WIDTH
    hd = SSM_HEAD_DIM
    n = SSM_STATE
    pairs = gw // (2 * hd)

    @pl.when(first)
    def _():
        st_ref[...] = jnp.zeros_like(st_ref)

    _conv_block(bufb_ref, b_ref, bh_ref, wb_ref, first, ts, n, lambda u, l: _silu(u + bb_ref[:, l]))
    _conv_block(bufc_ref, c_ref, ch_ref, wc_ref, first, ts, n, lambda u, l: _silu(u + bc_ref[:, l]))
    _conv_block(bufx_ref, x_ref, xh_ref, wx_ref, first, ts, 2 * n, lambda u, l: _silu(u + bx_ref[:, l]))

    sel_r = lax.broadcasted_iota(jnp.int32, (3 * GATE_COLS, gw), 0) % GATE_COLS
    sel_c = lax.broadcasted_iota(jnp.int32, (3 * GATE_COLS, gw), 1)
    for gg in range(SSD_GB):
        sel_ref[gg] = (sel_r == G_SSM_DT + (group0 + gg) * SSM_GROUP_HEADS + sel_c // hd).astype(BF16)

    ri = lax.broadcasted_iota(jnp.int32, (CHUNK, gw), 0)
    cm = lax.broadcasted_iota(jnp.int32, (CHUNK, gw), 1) % hd
    causal_w = cm <= ri
    upper_w = ri <= cm
    tri_b = _chunk_masks()[0].astype(BF16)
    lane_lo = lax.broadcasted_iota(jnp.int32, (CHUNK, 2 * hd), 1) < hd

    def chunk(c, carry):
        r0 = pl.multiple_of(c * CHUNK, CHUNK)
        rows = pl.ds(r0, CHUNK)
        g = g_ref[rows, :]
        groups = range(SSD_GB)
        xl = [slice(gg * gw, (gg + 1) * gw) for gg in groups]
        nl = [slice(gg * n, (gg + 1) * n) for gg in groups]
        g3 = _split3(g)
        cum3 = jnp.dot(tri_b, g3, preferred_element_type=F32)
        gcum = cum3[:, :GATE_COLS] + cum3[:, GATE_COLS:2 * GATE_COLS] + cum3[:, 2 * GATE_COLS:]
        both3 = jnp.concatenate([g3, _split3(gcum)], axis=0)
        expanded = [jnp.dot(both3, sel_ref[gg], preferred_element_type=F32) for gg in groups]
        neg_a = [-jnp.exp(alog_ref[:, xl[gg]]) for gg in groups]
        dt = [expanded[gg][:CHUNK] for gg in groups]
        a = [dt[gg] * neg_a[gg] for gg in groups]
        acs = [expanded[gg][CHUNK:] * neg_a[gg] for gg in groups]
        bm = [bufb_ref[rows, nl[gg]] for gg in groups]
        cmat = [bufc_ref[rows, nl[gg]] for gg in groups]
        cb2 = [_dot_nt(cmat[gg], jnp.concatenate([bm[gg], bm[gg]], axis=0)) for gg in groups]
        y_off = [_dot(cmat[gg], st_ref[gg]) for gg in groups]
        x = [bufx_ref[rows, xl[gg]] for gg in groups]
        xdt = [x[gg] * dt[gg] for gg in groups]
        acs_r = [jnp.sum(jnp.where(upper_w, a[gg], 0.0), axis=0, keepdims=True) for gg in groups]
        lm = [jnp.exp(jnp.where(causal_w, acs[gg] - acs_r[gg], NEG_BIG)) for gg in groups]
        mm = [(jnp.concatenate([cb2[gg]] * pairs, axis=1) * lm[gg]).astype(BF16) for gg in groups]
        y_diag = []
        for gg in groups:
            parts = []
            for p in range(pairs):
                xp = xdt[gg][:, p * 2 * hd:(p + 1) * 2 * hd]
                rhs = jnp.concatenate([jnp.where(lane_lo, xp, 0.0), jnp.where(lane_lo, 0.0, xp)], axis=0)
                parts.append(_dot(mm[gg][:, p * 2 * hd:(p + 1) * 2 * hd], rhs))
            y_diag.append(jnp.concatenate(parts, axis=1))
        for gg in groups:
            a_last = acs[gg][CHUNK - 1:CHUNK, :]
            st_ref[gg] = (st_ref[gg] * jnp.exp(a_last)
                          + _dot_tn(bm[gg], xdt[gg] * jnp.exp(a_last - acs[gg])))
        for gg in groups:
            y = y_diag[gg] + y_off[gg] * jnp.exp(acs[gg]) + dskip_ref[:, xl[gg]] * x[gg]
            y = y * _silu(z_ref[rows, xl[gg]])
            out_ref[rows, xl[gg]] = _rms_rows(y, nw_ref[:, xl[gg]]).astype(out_ref.dtype)
        return carry

    lax.fori_loop(0, ts // CHUNK, chunk, 0)


def _ssd(proj, gact, conv_w, conv_b, a_log, dskip, norm_w, batch, seq):
    ts = min(SSD_SEQ_BLOCK, seq)
    ns = seq // ts
    gw, n = SSD_GB * SSM_GROUP_WIDTH, SSD_GB * SSM_STATE
    row = lambda b, g, s: b * ns + s
    cur = lambda off, w: pl.BlockSpec((ts, w), lambda b, g, s: (row(b, g, s), off // w + g))
    halo = lambda off, w: pl.BlockSpec((HALO, w), lambda b, g, s: (_halo_row(b, s, ns, ts), off // w + g))
    par = lambda rows_, off, w: pl.BlockSpec((rows_, w), lambda b, g, s: (0, off // w + g))
    expand = lambda p: jnp.repeat(p, SSM_HEAD_DIM).reshape(1, SSM_WIDTH)
    conv_b = conv_b.reshape(1, -1)
    return pl.pallas_call(
        functools.partial(_ssd_kernel, ts=ts),
        grid=(batch, SSM_GROUPS // SSD_GB, ns),
        in_specs=[cur(P_SSM_Z, gw), cur(P_SSM_X, gw), cur(P_SSM_B, n), cur(P_SSM_C, n),
                  halo(P_SSM_X, gw), halo(P_SSM_B, n), halo(P_SSM_C, n),
                  pl.BlockSpec((ts, GATE_COLS), lambda b, g, s: (row(b, g, s), 0)),
                  par(CONV_K, 0, gw), par(CONV_K, P_SSM_B - P_SSM_X, n), par(CONV_K, P_SSM_C - P_SSM_X, n),
                  par(1, 0, gw), par(1, P_SSM_B - P_SSM_X, n), par(1, P_SSM_C - P_SSM_X, n),
                  par(1, 0, gw), par(1, 0, gw), par(1, 0, gw)],
        out_specs=pl.BlockSpec((ts, gw), lambda b, g, s: (row(b, g, s), g)),
        out_shape=jax.ShapeDtypeStruct((batch * seq, SSM_WIDTH), BF16),
        scratch_shapes=[pltpu.VMEM((SSD_GB, SSM_STATE, SSM_GROUP_WIDTH), F32),
                        pltpu.VMEM((ts, gw), F32), pltpu.VMEM((ts, n), F32), pltpu.VMEM((ts, n), F32),
                        pltpu.VMEM((SSD_GB, 3 * GATE_COLS, SSM_GROUP_WIDTH), BF16)],
        compiler_params=pltpu.CompilerParams(
            dimension_semantics=("parallel", "parallel", "arbitrary"),
            vmem_limit_bytes=VMEM_LIMIT),
        name="ssd_mixer",
    )(proj, proj, proj, proj, proj, proj, proj, gact, conv_w, conv_w, conv_w, conv_b, conv_b, conv_b,
      expand(a_log), expand(dskip), norm_w.reshape(1, SSM_WIDTH))


def _deepnorm_kernel(y_ref, x_ref, g_ref, b_ref, o_ref, obf_ref):
    u = DEEPNORM_ALPHA * x_ref[...] + y_ref[...]
    mu = jnp.mean(u, axis=-1, keepdims=True)
    var = jnp.mean(jnp.square(u - mu), axis=-1, keepdims=True)
    out = (u - mu) * lax.rsqrt(var + LN_EPS) * g_ref[...] + b_ref[...]
    o_ref[...] = out
    obf_ref[...] = out.astype(BF16)


def _deepnorm_gates_kernel(y_ref, x_ref, g_ref, b_ref, wml_ref, wgdn_ref, wssm_ref, bias_ref, alog_ref,
                           o_ref, obf_ref, gact_ref):
    u = DEEPNORM_ALPHA * x_ref[...] + y_ref[...]
    mu = jnp.mean(u, axis=-1, keepdims=True)
    var = jnp.mean(jnp.square(u - mu), axis=-1, keepdims=True)
    out = (u - mu) * lax.rsqrt(var + LN_EPS) * g_ref[...] + b_ref[...]
    o_ref[...] = out
    gact, out_bf = _gate_block(out, wml_ref, wgdn_ref, wssm_ref, bias_ref, alog_ref)
    obf_ref[...] = out_bf
    gact_ref[...] = gact


def _deepnorm_gates(y, x, ln_g, ln_b, w_t, next_layer, bias_row, alog_row, tm=256):
    m, d = x.shape
    tm = min(tm, m)
    blk = pl.BlockSpec((tm, d), lambda i: (i, 0))
    vec = pl.BlockSpec((1, d), lambda i: (0, 0))
    row = pl.BlockSpec((1, GATE_COLS), lambda i: (0, 0))
    return pl.pallas_call(
        _deepnorm_gates_kernel,
        grid=(m // tm,),
        in_specs=[blk, blk, vec, vec] + _gate_weight_specs(d, next_layer, None) + [row, row],
        out_specs=[blk, blk, pl.BlockSpec((tm, GATE_COLS), lambda i: (i, 0))],
        out_shape=[jax.ShapeDtypeStruct((m, d), F32), jax.ShapeDtypeStruct((m, d), BF16),
                   jax.ShapeDtypeStruct((m, GATE_COLS), F32)],
        compiler_params=pltpu.CompilerParams(dimension_semantics=("parallel",),
                                             vmem_limit_bytes=VMEM_LIMIT),
        name="deepnorm_ln_gates",
    )(y, x, ln_g.reshape(1, d), ln_b.reshape(1, d), w_t, w_t, w_t, bias_row, alog_row)


def _deepnorm(y, x, ln_g, ln_b, tm=256):
    m, d = x.shape
    tm = min(tm, m)
    blk = pl.BlockSpec((tm, d), lambda i: (i, 0))
    vec = pl.BlockSpec((1, d), lambda i: (0, 0))
    return pl.pallas_call(
        _deepnorm_kernel,
        grid=(m // tm,),
        in_specs=[blk, blk, vec, vec],
        out_specs=[blk, blk],
        out_shape=[jax.ShapeDtypeStruct((m, d), F32), jax.ShapeDtypeStruct((m, d), BF16)],
        compiler_params=pltpu.CompilerParams(dimension_semantics=("parallel",),
                                             vmem_limit_bytes=VMEM_LIMIT),
        name="deepnorm_ln",
    )(y, x, ln_g.reshape(1, d), ln_b.reshape(1, d))


def _gate_rows(ml_i_bias, ml_f_bias, gdn_dt_bias, gdn_a_log, ssm_dt_bias):
    zeros = lambda n: jnp.zeros((n,), F32)
    bias = jnp.concatenate([ml_i_bias, ml_f_bias, zeros(GDN_HEADS), gdn_dt_bias, ssm_dt_bias,
                            zeros(GATE_COLS - G_END)])
    alog = jnp.concatenate([zeros(G_GDN_A), gdn_a_log, zeros(GATE_COLS - G_SSM_DT)])
    return bias.reshape(1, GATE_COLS), alog.reshape(1, GATE_COLS)


def kernel(x, w_in, w_out, ml_i_bias, ml_f_bias, ml_norm_w, gdn_conv_w, gdn_A_log, gdn_dt_bias,
           gdn_norm_w, ssm_conv_w, ssm_conv_b, ssm_A_log, ssm_dt_bias, ssm_D, ssm_norm_w, ln_g, ln_b):
    batch, seq, d = x.shape
    xf = x.reshape(batch * seq, d).astype(F32)
    xb = None
    w_t = jnp.swapaxes(w_in.astype(F32), 1, 2)
    w_o = _cast_bf16(w_out.astype(F32))
    depth = w_in.shape[0]
    gate_rows = [_gate_rows(ml_i_bias[l], ml_f_bias[l], gdn_dt_bias[l], gdn_A_log[l], ssm_dt_bias[l])
                 for l in range(depth)]
    gact, xb = _gates(xf, w_t, 0, *gate_rows[0], True)
    for l in range(depth):
        y_ml = _mlstm(_proj(xb, w_t, l, 0, ML_MAIN), gact, ml_norm_w[l], batch, seq)
        y_gdn = _gdn(_proj(xb, w_t, l, ML_COLS, GDN_MAIN), gact, gdn_conv_w[l], gdn_norm_w[l],
                     batch, seq)
        y_ssm = _ssd(_proj(xb, w_t, l, ML_COLS + GDN_COLS, SSM_MAIN), gact, ssm_conv_w[l],
                     ssm_conv_b[l], ssm_A_log[l], ssm_D[l], ssm_norm_w[l], batch, seq)
        y = _out_matmul(y_ml, y_gdn, y_ssm, w_o, l)
        if l + 1 < depth:
            xf, xb, gact = _deepnorm_gates(y, xf, ln_g[l], ln_b[l], w_t, l + 1, *gate_rows[l + 1])
        else:
            xf, xb = _deepnorm(y, xf, ln_g[l], ln_b[l])
    return xf.reshape(batch, seq, d).astype(x.dtype)
```
